```python
import math
import jax, jax.numpy as jnp
from jax import lax
import numpy as np

D_MODEL = 4096
BATCH = 1
SEQ = 16384
DEPTH = 1

CHUNK = 64
PLE_DIM = 256
CONV_WIDTH = 3
CONV_CH = D_MODEL // 2
S5_CH = D_MODEL // 2
S5_GROUP_CH = 16
S5_GROUPS = S5_CH // S5_GROUP_CH
S5_STATE = 64
DT_MIN = 1e-3
DT_MAX = 1e-1
EPS = 1e-6
IN_COLS = 4 * CONV_CH + 2 * S5_CH + 2 * D_MODEL
SPLITS = (CONV_CH, 2 * CONV_CH, 3 * CONV_CH, 4 * CONV_CH,
          4 * CONV_CH + S5_CH, 4 * CONV_CH + 2 * S5_CH,
          4 * CONV_CH + 2 * S5_CH + D_MODEL)

kernel_name = "hybrid_shortconv_s5_gated_parallel"


def rmsnorm(x, g):
    x32 = x.astype(jnp.float32)
    y = x32 * lax.rsqrt(jnp.mean(x32 * x32, axis=-1, keepdims=True) + EPS)
    return (y * g.astype(jnp.float32)).astype(x.dtype)


def causal_dwconv(x, w, b):
    rhs = w[:, None, :].astype(x.dtype)
    y = lax.conv_general_dilated(x, rhs, window_strides=(1,), padding=[(CONV_WIDTH - 1, 0)],
                                 dimension_numbers=("NWC", "WIO", "NWC"),
                                 feature_group_count=x.shape[-1])
    return y + b.astype(x.dtype)


def s5_ssm(u, lam_re, lam_im, log_dt, b_re, b_im, c_re, c_im, d_skip):
    bsz, seq, _ = u.shape
    n_chunks = seq // CHUNK
    f32 = jnp.float32
    u32 = u.astype(f32)
    lam = lax.complex(lam_re.astype(f32), lam_im.astype(f32))
    dt = jnp.exp(log_dt.astype(f32))[:, None]
    lam_dt = lam * dt
    lam_bar = jnp.exp(lam_dt)
    b_mat = lax.complex(b_re.astype(f32), b_im.astype(f32))
    b_bar = ((lam_bar - 1.0) / lam)[..., None] * b_mat
    c_mat = lax.complex(c_re.astype(f32), c_im.astype(f32))
    steps = jnp.arange(1, CHUNK + 1, dtype=f32)
    carry_decay = jnp.exp(steps[:, None, None] * lam_dt[None])
    u_chunks = jnp.moveaxis(u32.reshape(bsz, n_chunks, CHUNK, S5_GROUPS, S5_GROUP_CH), 1, 0)

    def combine(left, right):
        a_l, b_l = left
        a_r, b_r = right
        return a_l * a_r, a_r * b_l + b_r

    def chunk_step(h0, u_c):
        bu = jnp.einsum("blgp,gnp->blgn", u_c.astype(jnp.complex64), b_bar)
        a = jnp.broadcast_to(lam_bar, bu.shape)
        _, h_local = lax.associative_scan(combine, (a, bu), axis=1)
        h = h_local + carry_decay[None] * h0[:, None]
        y = jnp.real(jnp.einsum("blgn,gpn->blgp", h, c_mat))
        return h[:, -1], y

    h_init = jnp.zeros((bsz, S5_GROUPS, S5_STATE), jnp.complex64)
    _, y = lax.scan(chunk_step, h_init, u_chunks)
    y = jnp.moveaxis(y, 0, 1).reshape(bsz, seq, S5_CH)
    return (y + d_skip.astype(f32) * u32).astype(u.dtype)


def setup_inputs(seed: int = 0) -> dict:
    key = jax.random.key(seed)
    ks = jax.random.split(key, 24)
    f32 = jnp.float32
    nrm = lambda k, shape, s: jax.random.normal(k, shape, f32) * s
    n_idx = jnp.arange(S5_STATE, dtype=f32)
    lam_re = -0.5 + nrm(ks[5], (DEPTH, S5_GROUPS, S5_STATE), 0.01)
    lam_im = math.pi * n_idx[None, None, :] + nrm(ks[6], (DEPTH, S5_GROUPS, S5_STATE), 0.01)
    log_dt = jax.random.uniform(ks[7], (DEPTH, S5_GROUPS), f32, math.log(DT_MIN), math.log(DT_MAX))
    bs = math.sqrt(0.5) * S5_GROUP_CH ** -0.5
    cs = math.sqrt(0.5) * S5_STATE ** -0.5
    return {
        "x": nrm(ks[0], (BATCH, SEQ, D_MODEL), 1.0),
        "p": nrm(ks[1], (DEPTH, BATCH, SEQ, PLE_DIM), 1.0),
        "norm_in_g": 1.0 + nrm(ks[2], (DEPTH, D_MODEL), 0.02),
        "w_in": nrm(ks[3], (DEPTH, D_MODEL, IN_COLS), D_MODEL ** -0.5),
        "conv_w": nrm(ks[4], (DEPTH, CONV_WIDTH, CONV_CH), CONV_WIDTH ** -0.5),
        "conv_b": nrm(ks[8], (DEPTH, CONV_CH), 0.02),
        "lam_re": lam_re,
        "lam_im": lam_im,
        "log_dt": log_dt,
        "b_re": nrm(ks[9], (DEPTH, S5_GROUPS, S5_STATE, S5_GROUP_CH), bs),
        "b_im": nrm(ks[10], (DEPTH, S5_GROUPS, S5_STATE, S5_GROUP_CH), bs),
        "c_re": nrm(ks[11], (DEPTH, S5_GROUPS, S5_GROUP_CH, S5_STATE), cs),
        "c_im": nrm(ks[12], (DEPTH, S5_GROUPS, S5_GROUP_CH, S5_STATE), cs),
        "d_skip": nrm(ks[13], (DEPTH, S5_CH), 1.0),
        "w_glu": nrm(ks[14], (DEPTH, S5_CH, S5_CH), S5_CH ** -0.5),
        "b_glu": nrm(ks[15], (DEPTH, S5_CH), 0.02),
        "w_up_conv": nrm(ks[16], (DEPTH, CONV_CH, D_MODEL), CONV_CH ** -0.5),
        "w_up_s5": nrm(ks[17], (DEPTH, S5_CH, D_MODEL), S5_CH ** -0.5),
        "w_out": nrm(ks[18], (DEPTH, D_MODEL, D_MODEL), D_MODEL ** -0.5),
        "ple_norm_g": 1.0 + nrm(ks[19], (DEPTH, D_MODEL), 0.02),
        "w_ple_gate": nrm(ks[20], (DEPTH, D_MODEL, D_MODEL), D_MODEL ** -0.5),
        "w_ple_proj": nrm(ks[21], (DEPTH, PLE_DIM, D_MODEL), PLE_DIM ** -0.5),
        "final_norm_g": 1.0 + nrm(ks[22], (D_MODEL,), 0.02),
    }


def reference(x, p, norm_in_g, w_in, conv_w, conv_b, lam_re, lam_im, log_dt, b_re, b_im,
              c_re, c_im, d_skip, w_glu, b_glu, w_up_conv, w_up_s5, w_out, ple_norm_g,
              w_ple_gate, w_ple_proj, final_norm_g):
    h = x
    for i in range(DEPTH):
        hn = rmsnorm(h, norm_in_g[i])
        proj = hn @ w_in[i]
        xa, ca, ba, za, ub, zb, ga, gb = jnp.split(proj, SPLITS, axis=-1)
        ya = ba * causal_dwconv(ca * xa, conv_w[i], conv_b[i])
        ya = ya * jax.nn.silu(za)
        yb = s5_ssm(ub, lam_re[i], lam_im[i], log_dt[i], b_re[i], b_im[i], c_re[i], c_im[i], d_skip[i])
        yb = jax.nn.gelu(yb)
        yb = yb * jax.nn.sigmoid(yb @ w_glu[i] + b_glu[i])
        yb = yb * jax.nn.silu(zb)
        merged = jax.nn.sigmoid(ga) * (ya @ w_up_conv[i]) + jax.nn.sigmoid(gb) * (yb @ w_up_s5[i])
        h = h + merged @ w_out[i]
        hp = rmsnorm(h, ple_norm_g[i])
        h = h + jax.nn.sigmoid(hp @ w_ple_gate[i]) * (p[i] @ w_ple_proj[i])
    return rmsnorm(h, final_norm_g)
```

```python
import functools

import numpy as np
import jax
import jax.numpy as jnp
from jax import lax
from jax.experimental import pallas as pl
from jax.experimental.pallas import tpu as pltpu

EPS = 1e-6
SUBLANES = 8
SCAN_CHUNK = 64
PERM_TILE = SUBLANES * SCAN_CHUNK
S5_GROUP_CH = 16
S5_STATE = 64
GROUPS_PER_BLOCK = 16
BLOCK_CH = GROUPS_PER_BLOCK * S5_GROUP_CH
BLOCK_ST = GROUPS_PER_BLOCK * S5_STATE
SCAN_LANES = 512
VMEM_LIMIT = 56 * 1024 * 1024

_BF16 = jnp.bfloat16
_F32 = jnp.float32


def _dot(a, b):
    return jnp.dot(a, b, preferred_element_type=_F32)


def _sigmoid(v):
    return jax.nn.sigmoid(v)


def _silu(v):
    return v * jax.nn.sigmoid(v)


def _perm_matrix():
    pm = np.zeros((PERM_TILE, PERM_TILE), np.float32)
    for j in range(SUBLANES):
        for l in range(SCAN_CHUNK):
            pm[l * SUBLANES + j, j * SCAN_CHUNK + l] = 1.0
    return pm


def _params(*sem):
    return pltpu.CompilerParams(dimension_semantics=sem, vmem_limit_bytes=VMEM_LIMIT)


def _prep_kernel(x_ref, g_ref, pm_ref, o_ref):
    x = x_ref[...]
    ms = jnp.mean(x * x, axis=-1, keepdims=True)
    hn = (x * lax.rsqrt(ms + EPS) * g_ref[...]).astype(_BF16)
    o_ref[...] = _dot(pm_ref[...], hn).astype(_BF16)


def _prep(x2, gain, pm):
    s, d = x2.shape
    return pl.pallas_call(
        _prep_kernel,
        grid=(s // PERM_TILE,),
        in_specs=[pl.BlockSpec((PERM_TILE, d), lambda i: (i, 0)),
                  pl.BlockSpec((1, d), lambda i: (0, 0)),
                  pl.BlockSpec((PERM_TILE, PERM_TILE), lambda i: (0, 0))],
        out_specs=pl.BlockSpec((PERM_TILE, d), lambda i: (i, 0)),
        out_shape=jax.ShapeDtypeStruct((s, d), _BF16),
        compiler_params=_params("parallel"),
        name="prep_norm_permute",
    )(x2, gain, pm)


def _branch_a_kernel(hn_ref, wx_ref, wc_ref, wb_ref, wz_ref, cw_ref, cb_ref, o_ref, tail_ref):
    i = pl.program_id(1)

    @pl.when(i == 0)
    def _():
        tail_ref[...] = jnp.zeros_like(tail_ref)

    hn = hn_ref[...]
    v = _dot(hn, wc_ref[...]) * _dot(hn, wx_ref[...])
    tm, tn = v.shape
    tail = v[tm - 2 * SUBLANES:, :]
    prev = tail_ref[...]
    sub = lax.broadcasted_iota(jnp.int32, (SUBLANES, tn), 0)

    def from_previous_chunk(cur, prev_last):
        return jnp.where(sub == 0, prev_last, pltpu.roll(cur, 1, 0))

    r62 = from_previous_chunk(tail[:SUBLANES], prev[SUBLANES - 1:SUBLANES])
    r63 = from_previous_chunk(tail[SUBLANES:], prev[2 * SUBLANES - 1:])
    tail_ref[...] = tail
    vs1 = jnp.concatenate([r63, v[:tm - SUBLANES]], axis=0)
    vs2 = jnp.concatenate([r62, r63, v[:tm - 2 * SUBLANES]], axis=0)
    cw = cw_ref[...]
    conv = cw[0:1] * vs2 + cw[1:2] * vs1 + cw[2:3] * v + cb_ref[...]
    ba = _dot(hn, wb_ref[...])
    za = _dot(hn, wz_ref[...])
    o_ref[...] = (ba * conv * _silu(za)).astype(_BF16)


def _branch_a(hn, w_in, conv_w, conv_b, conv_ch, tn):
    s, d = hn.shape
    nj = conv_ch // tn
    wspec = lambda k: pl.BlockSpec((d, tn), lambda j, i, k=k: (0, k * nj + j))
    return pl.pallas_call(
        _branch_a_kernel,
        grid=(nj, s // PERM_TILE),
        in_specs=[pl.BlockSpec((PERM_TILE, d), lambda j, i: (i, 0)),
                  wspec(0), wspec(1), wspec(2), wspec(3),
                  pl.BlockSpec((conv_w.shape[0], tn), lambda j, i: (0, j)),
                  pl.BlockSpec((1, tn), lambda j, i: (0, j))],
        out_specs=pl.BlockSpec((PERM_TILE, tn), lambda j, i: (i, j)),
        out_shape=jax.ShapeDtypeStruct((s, conv_ch), _BF16),
        scratch_shapes=[pltpu.VMEM((2 * SUBLANES, tn), _F32)],
        compiler_params=_params("parallel", "arbitrary"),
        name="branch_a_conv",
    )(hn, w_in, w_in, w_in, w_in, conv_w, conv_b)


def _s5_kernel(hn_ref, wu_ref, wb_ref, wc_ref, lam_ref, tab_ref, d_ref, o_ref,
               bu_ref, u_ref, st_ref):
    i = pl.program_id(1)
    w = BLOCK_ST

    @pl.when(i == 0)
    def _():
        st_ref[...] = jnp.zeros_like(st_ref)

    u = _dot(hn_ref[...], wu_ref[...])
    u_ref[...] = u
    bu_ref[...] = _dot(u.astype(_BF16), wb_ref[0])

    for s in range(w // SCAN_LANES):
        re = pl.ds(s * SCAN_LANES, SCAN_LANES)
        im = pl.ds(w + s * SCAN_LANES, SCAN_LANES)
        shape = (SUBLANES, SCAN_LANES)
        lr = jnp.broadcast_to(lam_ref[0, :, re], shape)
        li = jnp.broadcast_to(lam_ref[0, :, im], shape)

        def local_step(l, carry):
            hr, hi = carry
            rows = pl.ds(pl.multiple_of(l * SUBLANES, SUBLANES), SUBLANES)
            nr = lr * hr - li * hi + bu_ref[rows, re]
            ni = lr * hi + li * hr + bu_ref[rows, im]
            bu_ref[rows, re] = nr
            bu_ref[rows, im] = ni
            return nr, ni

        zero = jnp.zeros(shape, _F32)
        er, ei = lax.fori_loop(0, SCAN_CHUNK, local_step, (zero, zero), unroll=8)

        pr = tab_ref[0, SCAN_CHUNK - 1:SCAN_CHUNK, re]
        pi = tab_ref[0, SCAN_CHUNK - 1:SCAN_CHUNK, im]
        cr = st_ref[0:1, re]
        ci = st_ref[0:1, im]
        rows_r, rows_i = [], []
        for j in range(SUBLANES):
            rows_r.append(cr)
            rows_i.append(ci)
            cr, ci = (er[j:j + 1] + (pr * cr - pi * ci),
                      ei[j:j + 1] + (pr * ci + pi * cr))
        st_ref[0:1, re] = cr
        st_ref[0:1, im] = ci
        cin_r = jnp.concatenate(rows_r, axis=0)
        cin_i = jnp.concatenate(rows_i, axis=0)

        def fix_step(l, carry):
            rows = pl.ds(pl.multiple_of(l * SUBLANES, SUBLANES), SUBLANES)
            tr = jnp.broadcast_to(tab_ref[0, pl.ds(l, 1), re], shape)
            ti = jnp.broadcast_to(tab_ref[0, pl.ds(l, 1), im], shape)
            bu_ref[rows, re] = bu_ref[rows, re] + (tr * cin_r - ti * cin_i)
            bu_ref[rows, im] = bu_ref[rows, im] + (tr * cin_i + ti * cin_r)
            return carry

        lax.fori_loop(0, SCAN_CHUNK, fix_step, 0, unroll=8)

    y = _dot(bu_ref[...].astype(_BF16), wc_ref[0])
    y = y + d_ref[...] * u_ref[...]
    o_ref[...] = jax.nn.gelu(y).astype(_BF16)


def _s5(hn, w_in, ub_col0, wb, wc, lam, tab, d_skip, s5_ch):
    s, d = hn.shape
    nb = s5_ch // BLOCK_CH
    off = ub_col0 // BLOCK_CH
    return pl.pallas_call(
        _s5_kernel,
        grid=(nb, s // PERM_TILE),
        in_specs=[pl.BlockSpec((PERM_TILE, d), lambda b, i: (i, 0)),
                  pl.BlockSpec((d, BLOCK_CH), lambda b, i: (0, off + b)),
                  pl.BlockSpec((1, BLOCK_CH, 2 * BLOCK_ST), lambda b, i: (b, 0, 0)),
                  pl.BlockSpec((1, 2 * BLOCK_ST, BLOCK_CH), lambda b, i: (b, 0, 0)),
                  pl.BlockSpec((1, 1, 2 * BLOCK_ST), lambda b, i: (b, 0, 0)),
                  pl.BlockSpec((1, SCAN_CHUNK, 2 * BLOCK_ST), lambda b, i: (b, 0, 0)),
                  pl.BlockSpec((1, BLOCK_CH), lambda b, i: (0, b))],
        out_specs=pl.BlockSpec((PERM_TILE, BLOCK_CH), lambda b, i: (i, b)),
        out_shape=jax.ShapeDtypeStruct((s, s5_ch), _BF16),
        scratch_shapes=[pltpu.VMEM((PERM_TILE, 2 * BLOCK_ST), _F32),
                        pltpu.VMEM((PERM_TILE, BLOCK_CH), _F32),
                        pltpu.VMEM((SUBLANES, 2 * BLOCK_ST), _F32)],
        compiler_params=_params("parallel", "arbitrary"),
        name="branch_b_s5",
    )(hn, w_in, wb, wc, lam, tab, d_skip)


def _s5_params(lam_re, lam_im, log_dt, b_re, b_im, c_re, c_im):
    g = lam_re.shape[0]
    nb = g // GROUPS_PER_BLOCK
    lam = lax.complex(lam_re, lam_im)
    lam_dt = lam * jnp.exp(log_dt)[:, None]
    lam_bar = jnp.exp(lam_dt)
    b_bar = ((lam_bar - 1.0) / lam)[..., None] * lax.complex(b_re, b_im)
    steps = jnp.arange(1, SCAN_CHUNK + 1, dtype=_F32)
    decay = jnp.exp(steps[:, None, None] * lam_dt[None])
    eye = jnp.eye(GROUPS_PER_BLOCK, dtype=_F32)

    def in_mat(m):
        m = m.reshape(nb, GROUPS_PER_BLOCK, S5_STATE, S5_GROUP_CH)
        return jnp.einsum("bgnp,gh->bgphn", m, eye).reshape(nb, BLOCK_CH, BLOCK_ST)

    def out_mat(m):
        m = m.reshape(nb, GROUPS_PER_BLOCK, S5_GROUP_CH, S5_STATE)
        return jnp.einsum("bgqn,gh->bhngq", m, eye).reshape(nb, BLOCK_ST, BLOCK_CH)

    wb = jnp.concatenate([in_mat(jnp.real(b_bar)), in_mat(jnp.imag(b_bar))], axis=2).astype(_BF16)
    wc = jnp.concatenate([out_mat(c_re), out_mat(-c_im)], axis=1).astype(_BF16)

    def lanes(m):
        lead = m.shape[:-2]
        m = m.reshape(lead + (nb, BLOCK_ST))
        return jnp.moveaxis(m, -2, 0)

    lam_l = jnp.concatenate([lanes(jnp.real(lam_bar)), lanes(jnp.imag(lam_bar))], axis=-1)[:, None, :]
    tab = jnp.concatenate([lanes(jnp.real(decay)), lanes(jnp.imag(decay))], axis=-1)
    return wb, wc, lam_l, tab


def _glu_kernel(g_ref, gj_ref, hn_ref, wg_ref, bg_ref, wz_ref, o_ref):
    gate = _dot(g_ref[...], wg_ref[...]) + bg_ref[...]
    zb = _dot(hn_ref[...], wz_ref[...])
    o_ref[...] = (gj_ref[...].astype(_F32) * _sigmoid(gate) * _silu(zb)).astype(_BF16)


def _glu(g, hn, w_glu, b_glu, w_in, zb_col0, tm, tn):
    s, c = g.shape
    d = hn.shape[1]
    off = zb_col0 // tn
    return pl.pallas_call(
        _glu_kernel,
        grid=(c // tn, s // tm),
        in_specs=[pl.BlockSpec((tm, c), lambda j, i: (i, 0)),
                  pl.BlockSpec((tm, tn), lambda j, i: (i, j)),
                  pl.BlockSpec((tm, d), lambda j, i: (i, 0)),
                  pl.BlockSpec((c, tn), lambda j, i: (0, j)),
                  pl.BlockSpec((1, tn), lambda j, i: (0, j)),
                  pl.BlockSpec((d, tn), lambda j, i: (0, off + j))],
        out_specs=pl.BlockSpec((tm, tn), lambda j, i: (i, j)),
        out_shape=jax.ShapeDtypeStruct((s, c), _BF16),
        compiler_params=_params("parallel", "parallel"),
        name="branch_b_glu",
    )(g, g, hn, w_glu, b_glu, w_in)


def _merge_kernel(hn_ref, ya_ref, yb_ref, wga_ref, wgb_ref, wua_ref, wub_ref, pt_ref, o_ref):
    hn = hn_ref[...]
    m = _sigmoid(_dot(hn, wga_ref[...])) * _dot(ya_ref[...], wua_ref[...])
    m = m + _sigmoid(_dot(hn, wgb_ref[...])) * _dot(yb_ref[...], wub_ref[...])
    o_ref[...] = _dot(pt_ref[...], m.astype(_BF16)).astype(_BF16)


def _merge(hn, ya, yb, w_in, ga_col0, w_up_a, w_up_b, pt, tn):
    s, d = hn.shape
    ca, cb = ya.shape[1], yb.shape[1]
    nj = d // tn
    off_a = ga_col0 // tn
    off_b = off_a + nj
    return pl.pallas_call(
        _merge_kernel,
        grid=(nj, s // PERM_TILE),
        in_specs=[pl.BlockSpec((PERM_TILE, d), lambda j, i: (i, 0)),
                  pl.BlockSpec((PERM_TILE, ca), lambda j, i: (i, 0)),
                  pl.BlockSpec((PERM_TILE, cb), lambda j, i: (i, 0)),
                  pl.BlockSpec((d, tn), lambda j, i: (0, off_a + j)),
                  pl.BlockSpec((d, tn), lambda j, i: (0, off_b + j)),
                  pl.BlockSpec((ca, tn), lambda j, i: (0, j)),
                  pl.BlockSpec((cb, tn), lambda j, i: (0, j)),
                  pl.BlockSpec((PERM_TILE, PERM_TILE), lambda j, i: (0, 0))],
        out_specs=pl.BlockSpec((PERM_TILE, tn), lambda j, i: (i, j)),
        out_shape=jax.ShapeDtypeStruct((s, d), _BF16),
        compiler_params=_params("parallel", "parallel"),
        name="merge_unpermute",
    )(hn, ya, yb, w_in, w_in, w_up_a, w_up_b, pt)


def _out_kernel(m_ref, x_ref, wo_ref, wpg_ref, p_ref, wpp_ref, gp_ref, gf_ref, o_ref, hp_ref,
                *, n1, tn, final_norm):
    j = pl.program_id(1)
    nj = pl.num_programs(1)

    def rms(v, gain):
        ms = jnp.mean(v * v, axis=-1, keepdims=True)
        return v * lax.rsqrt(ms + EPS) * gain

    @pl.when(j < n1)
    def _():
        cols = pl.ds(pl.multiple_of(j * tn, tn), tn)
        o_ref[:, cols] = x_ref[...] + _dot(m_ref[...], wo_ref[...])

    @pl.when(j == n1 - 1)
    def _():
        hp_ref[...] = rms(o_ref[...], gp_ref[...]).astype(_BF16)

    @pl.when(j >= n1)
    def _():
        cols = pl.ds(pl.multiple_of((j - n1) * tn, tn), tn)
        gate = _sigmoid(_dot(hp_ref[...], wpg_ref[...]))
        emb = _dot(p_ref[...].astype(_BF16), wpp_ref[...])
        o_ref[:, cols] = o_ref[:, cols] + gate * emb

    if final_norm:
        @pl.when(j == nj - 1)
        def _():
            o_ref[...] = rms(o_ref[...], gf_ref[...])


def _out(merged, x2, w_out, w_pg, p2, w_pp, g_ple, g_final, tm, tn, final_norm):
    s, d = x2.shape
    pd = p2.shape[1]
    n1 = d // tn
    first = lambda j: jnp.minimum(j, n1 - 1)
    second = lambda j: jnp.maximum(j - n1, 0)
    return pl.pallas_call(
        functools.partial(_out_kernel, n1=n1, tn=tn, final_norm=final_norm),
        grid=(s // tm, 2 * n1),
        in_specs=[pl.BlockSpec((tm, d), lambda i, j: (i, 0)),
                  pl.BlockSpec((tm, tn), lambda i, j: (i, first(j))),
                  pl.BlockSpec((d, tn), lambda i, j: (0, first(j))),
                  pl.BlockSpec((d, tn), lambda i, j: (0, second(j))),
                  pl.BlockSpec((tm, pd), lambda i, j: (i, 0)),
                  pl.BlockSpec((pd, tn), lambda i, j: (0, second(j))),
                  pl.BlockSpec((1, d), lambda i, j: (0, 0)),
                  pl.BlockSpec((1, d), lambda i, j: (0, 0))],
        out_specs=pl.BlockSpec((tm, d), lambda i, j: (i, 0)),
        out_shape=jax.ShapeDtypeStruct((s, d), _F32),
        scratch_shapes=[pltpu.VMEM((tm, d), _BF16)],
        compiler_params=_params("parallel", "arbitrary"),
        name="out_proj_ple_norm",
    )(merged, x2, w_out, w_pg, p2, w_pp, g_ple, g_final)


def kernel(x, p, norm_in_g, w_in, conv_w, conv_b, lam_re, lam_im, log_dt, b_re, b_im, c_re, c_im,
           d_skip, w_glu, b_glu, w_up_conv, w_up_s5, w_out, ple_norm_g, w_ple_gate, w_ple_proj,
           final_norm_g):
    bsz, seq, d = x.shape
    depth = w_in.shape[0]
    conv_ch = conv_w.shape[-1]
    s5_ch = d_skip.shape[-1]
    assert bsz == 1 and seq % PERM_TILE == 0 and s5_ch % BLOCK_CH == 0
    col_ub = 4 * conv_ch
    col_zb = col_ub + s5_ch
    col_ga = col_zb + s5_ch
    tn_a = min(256, conv_ch)
    tn = min(512, conv_ch, s5_ch, d)

    pm_np = _perm_matrix()
    pm = jnp.asarray(pm_np, _BF16)
    pt = jnp.asarray(pm_np.T, _BF16)

    h = x.reshape(seq, d)
    for i in range(depth):
        last = i == depth - 1
        w_in_b = w_in[i].astype(_BF16)
        wb, wc, lam_l, tab = _s5_params(lam_re[i], lam_im[i], log_dt[i], b_re[i], b_im[i],
                                        c_re[i], c_im[i])
        hn = _prep(h, norm_in_g[i][None, :], pm)
        ya = _branch_a(hn, w_in_b, conv_w[i], conv_b[i][None, :], conv_ch, tn_a)
        g = _s5(hn, w_in_b, col_ub, wb, wc, lam_l, tab, d_skip[i][None, :], s5_ch)
        yb = _glu(g, hn, w_glu[i].astype(_BF16), b_glu[i][None, :], w_in_b, col_zb, PERM_TILE, tn)
        merged = _merge(hn, ya, yb, w_in_b, col_ga, w_up_conv[i].astype(_BF16),
                        w_up_s5[i].astype(_BF16), pt, tn)
        h = _out(merged, h, w_out[i].astype(_BF16), w_ple_gate[i].astype(_BF16),
                 p[i].reshape(seq, -1), w_ple_proj[i].astype(_BF16), ple_norm_g[i][None, :],
                 final_norm_g[None, :], PERM_TILE, tn, last)
    return h.reshape(bsz, seq, d)
```

```python
import functools

import numpy as np
import jax
import jax.numpy as jnp
from jax import lax
from jax.experimental import pallas as pl
from jax.experimental.pallas import tpu as pltpu

EPS = 1e-6
SUBLANES = 8
SCAN_CHUNK = 64
PERM_TILE = SUBLANES * SCAN_CHUNK
S5_GROUP_CH = 16
S5_STATE = 64
GROUPS_PER_HALF = 16
HALF_CH = GROUPS_PER_HALF * S5_GROUP_CH
HALF_ST = GROUPS_PER_HALF * S5_STATE
HALVES = 2
SCAN_LANES = 512
VMEM_LIMIT = 60 * 1024 * 1024

_BF16 = jnp.bfloat16
_F32 = jnp.float32


def _dot(a, b):
    return jnp.dot(a, b, preferred_element_type=_F32)


def _sigmoid(v):
    return jax.nn.sigmoid(v)


def _silu(v):
    return v * jax.nn.sigmoid(v)


def _perm_matrix():
    pm = np.zeros((PERM_TILE, PERM_TILE), np.float32)
    for j in range(SUBLANES):
        for l in range(SCAN_CHUNK):
            pm[l * SUBLANES + j, j * SCAN_CHUNK + l] = 1.0
    return pm


def _params(*sem):
    return pltpu.CompilerParams(dimension_semantics=sem, vmem_limit_bytes=VMEM_LIMIT)


def _resident(shape, index_map):
    return pl.BlockSpec(shape, index_map, pipeline_mode=pl.Buffered(1))


def _prep_kernel(x_ref, g_ref, pm_ref, o_ref):
    x = x_ref[...]
    ms = jnp.mean(x * x, axis=-1, keepdims=True)
    hn = (x * lax.rsqrt(ms + EPS) * g_ref[...]).astype(_BF16)
    o_ref[...] = _dot(pm_ref[...], hn).astype(_BF16)


def _prep(x2, gain, pm):
    s, d = x2.shape
    return pl.pallas_call(
        _prep_kernel,
        grid=(s // PERM_TILE,),
        in_specs=[pl.BlockSpec((PERM_TILE, d), lambda i: (i, 0)),
                  pl.BlockSpec((1, d), lambda i: (0, 0)),
                  pl.BlockSpec((PERM_TILE, PERM_TILE), lambda i: (0, 0))],
        out_specs=pl.BlockSpec((PERM_TILE, d), lambda i: (i, 0)),
        out_shape=jax.ShapeDtypeStruct((s, d), _BF16),
        compiler_params=_params("parallel"),
        name="prep_norm_permute",
    )(x2, gain, pm)


def _branch_a_kernel(hn_ref, wx_ref, wc_ref, wb_ref, wz_ref, cw_ref, cb_ref, o_ref,
                     v_ref, tail_ref):
    i = pl.program_id(1)

    @pl.when(i == 0)
    def _():
        tail_ref[...] = jnp.zeros_like(tail_ref)

    hn = hn_ref[...]
    v_ref[...] = _dot(hn, wc_ref[...]) * _dot(hn, wx_ref[...])
    gate = _dot(hn, wb_ref[...]) * _silu(_dot(hn, wz_ref[...]))
    tm, tn = v_ref.shape
    sub = lax.broadcasted_iota(jnp.int32, (SUBLANES, tn), 0)
    cw = cw_ref[...]

    def from_previous_chunk(cur, prev_last):
        return jnp.where(sub == 0, prev_last, pltpu.roll(cur, 1, 0))

    for t in range(tm // PERM_TILE):
        r0 = t * PERM_TILE
        v = v_ref[r0:r0 + PERM_TILE, :]
        tail = v[PERM_TILE - 2 * SUBLANES:, :]
        prev = tail_ref[...]
        r62 = from_previous_chunk(tail[:SUBLANES], prev[SUBLANES - 1:SUBLANES])
        r63 = from_previous_chunk(tail[SUBLANES:], prev[2 * SUBLANES - 1:])
        tail_ref[...] = tail
        vs1 = jnp.concatenate([r63, v[:PERM_TILE - SUBLANES]], axis=0)
        vs2 = jnp.concatenate([r62, r63, v[:PERM_TILE - 2 * SUBLANES]], axis=0)
        conv = cw[0:1] * vs2 + cw[1:2] * vs1 + cw[2:3] * v + cb_ref[...]
        o_ref[r0:r0 + PERM_TILE, :] = (gate[r0:r0 + PERM_TILE, :] * conv).astype(_BF16)


def _branch_a(hn, w_in, conv_w, conv_b, conv_ch, tm, tn):
    s, d = hn.shape
    nj = conv_ch // tn
    wspec = lambda k: _resident((d, tn), lambda j, i, k=k: (0, k * nj + j))
    return pl.pallas_call(
        _branch_a_kernel,
        grid=(nj, s // tm),
        in_specs=[pl.BlockSpec((tm, d), lambda j, i: (i, 0)),
                  wspec(0), wspec(1), wspec(2), wspec(3),
                  _resident((conv_w.shape[0], tn), lambda j, i: (0, j)),
                  _resident((1, tn), lambda j, i: (0, j))],
        out_specs=pl.BlockSpec((tm, tn), lambda j, i: (i, j)),
        out_shape=jax.ShapeDtypeStruct((s, conv_ch), _BF16),
        scratch_shapes=[pltpu.VMEM((tm, tn), _F32),
                        pltpu.VMEM((2 * SUBLANES, tn), _F32)],
        compiler_params=_params("parallel", "arbitrary"),
        name="branch_a_conv",
    )(hn, w_in, w_in, w_in, w_in, conv_w, conv_b)


def _s5_kernel(hn_ref, wu_ref, wb_ref, wc_ref, lam_ref, p64_ref, d_ref, o_ref,
               bu_ref, hb_ref, st_ref):
    i = pl.program_id(1)
    w = HALF_ST
    shape = (SUBLANES, SCAN_LANES)

    @pl.when(i == 0)
    def _():
        st_ref[...] = jnp.zeros_like(st_ref)

    u = _dot(hn_ref[...], wu_ref[...])
    ub = u.astype(_BF16)
    for h in range(HALVES):
        bu_ref[h] = _dot(ub[:, h * HALF_CH:(h + 1) * HALF_CH], wb_ref[h])

    for h in range(HALVES):
        for s in range(w // SCAN_LANES):
            re = pl.ds(s * SCAN_LANES, SCAN_LANES)
            im = pl.ds(w + s * SCAN_LANES, SCAN_LANES)
            lr = jnp.broadcast_to(lam_ref[h, :, re], shape)
            li = jnp.broadcast_to(lam_ref[h, :, im], shape)

            def advance(hr, hi, rows):
                return (lr * hr - li * hi + bu_ref[h, rows, re],
                        lr * hi + li * hr + bu_ref[h, rows, im])

            def end_step(l, carry):
                rows = pl.ds(pl.multiple_of(l * SUBLANES, SUBLANES), SUBLANES)
                return advance(*carry, rows)

            zero = jnp.zeros(shape, _F32)
            er, ei = lax.fori_loop(0, SCAN_CHUNK, end_step, (zero, zero), unroll=8)

            pr = p64_ref[h, :, re]
            pi = p64_ref[h, :, im]
            cr = st_ref[h, 0:1, re]
            ci = st_ref[h, 0:1, im]
            rows_r, rows_i = [], []
            for j in range(SUBLANES):
                rows_r.append(cr)
                rows_i.append(ci)
                cr, ci = (er[j:j + 1] + (pr * cr - pi * ci),
                          ei[j:j + 1] + (pr * ci + pi * cr))
            st_ref[h, 0:1, re] = cr
            st_ref[h, 0:1, im] = ci
            cin = (jnp.concatenate(rows_r, axis=0), jnp.concatenate(rows_i, axis=0))

            def scan_step(l2, carry):
                r0 = pl.multiple_of(l2 * 2 * SUBLANES, 2 * SUBLANES)
                ar, ai = advance(*carry, pl.ds(r0, SUBLANES))
                br, bi = advance(ar, ai, pl.ds(r0 + SUBLANES, SUBLANES))
                rows = pl.ds(r0, 2 * SUBLANES)
                hb_ref[h, rows, re] = jnp.concatenate([ar, br], axis=0).astype(_BF16)
                hb_ref[h, rows, im] = jnp.concatenate([ai, bi], axis=0).astype(_BF16)
                return br, bi

            lax.fori_loop(0, SCAN_CHUNK // 2, scan_step, cin, unroll=4)

    y = jnp.concatenate([_dot(hb_ref[h], wc_ref[h]) for h in range(HALVES)], axis=1)
    o_ref[...] = jax.nn.gelu(y + d_ref[...] * u).astype(_BF16)


def _s5(hn, w_in, ub_col0, wb, wc, lam, p64, d_skip, s5_ch):
    s, d = hn.shape
    bc = HALVES * HALF_CH
    nb = s5_ch // bc
    off = ub_col0 // bc
    return pl.pallas_call(
        _s5_kernel,
        grid=(nb, s // PERM_TILE),
        in_specs=[pl.BlockSpec((PERM_TILE, d), lambda b, i: (i, 0)),
                  _resident((d, bc), lambda b, i: (0, off + b)),
                  _resident((HALVES, HALF_CH, 2 * HALF_ST), lambda b, i: (b, 0, 0)),
                  _resident((HALVES, 2 * HALF_ST, HALF_CH), lambda b, i: (b, 0, 0)),
                  _resident((HALVES, 1, 2 * HALF_ST), lambda b, i: (b, 0, 0)),
                  _resident((HALVES, 1, 2 * HALF_ST), lambda b, i: (b, 0, 0)),
                  _resident((1, bc), lambda b, i: (0, b))],
        out_specs=pl.BlockSpec((PERM_TILE, bc), lambda b, i: (i, b)),
        out_shape=jax.ShapeDtypeStruct((s, s5_ch), _BF16),
        scratch_shapes=[pltpu.VMEM((HALVES, PERM_TILE, 2 * HALF_ST), _F32),
                        pltpu.VMEM((HALVES, PERM_TILE, 2 * HALF_ST), _BF16),
                        pltpu.VMEM((HALVES, SUBLANES, 2 * HALF_ST), _F32)],
        compiler_params=_params("parallel", "arbitrary"),
        name="branch_b_s5",
    )(hn, w_in, wb, wc, lam, p64, d_skip)


def _s5_params(lam_re, lam_im, log_dt, b_re, b_im, c_re, c_im):
    g = lam_re.shape[0]
    nh = g // GROUPS_PER_HALF
    dt = jnp.exp(log_dt)[:, None]
    a, b = lam_re * dt, lam_im * dt
    mag = jnp.exp(a)
    lbr, lbi = mag * jnp.cos(b), mag * jnp.sin(b)
    den = lam_re * lam_re + lam_im * lam_im
    qr = ((lbr - 1.0) * lam_re + lbi * lam_im) / den
    qi = (lbi * lam_re - (lbr - 1.0) * lam_im) / den
    bbr = qr[..., None] * b_re - qi[..., None] * b_im
    bbi = qr[..., None] * b_im + qi[..., None] * b_re
    mag64 = jnp.exp(SCAN_CHUNK * a)
    p64r, p64i = mag64 * jnp.cos(SCAN_CHUNK * b), mag64 * jnp.sin(SCAN_CHUNK * b)
    eye = jnp.eye(GROUPS_PER_HALF, dtype=_F32)

    def in_mat(m):
        m = m.reshape(nh, GROUPS_PER_HALF, S5_STATE, S5_GROUP_CH)
        return jnp.einsum("bgnp,gh->bgphn", m, eye).reshape(nh, HALF_CH, HALF_ST)

    def out_mat(m):
        m = m.reshape(nh, GROUPS_PER_HALF, S5_GROUP_CH, S5_STATE)
        return jnp.einsum("bgqn,gh->bhngq", m, eye).reshape(nh, HALF_ST, HALF_CH)

    wb = jnp.concatenate([in_mat(bbr), in_mat(bbi)], axis=2).astype(_BF16)
    wc = jnp.concatenate([out_mat(c_re), out_mat(-c_im)], axis=1).astype(_BF16)
    lanes = lambda r, m: jnp.concatenate([r.reshape(nh, 1, HALF_ST), m.reshape(nh, 1, HALF_ST)], axis=2)
    return wb, wc, lanes(lbr, lbi), lanes(p64r, p64i)


def _glu_kernel(g_ref, gj_ref, hn_ref, wg_ref, bg_ref, wz_ref, o_ref):
    gate = _sigmoid(_dot(g_ref[...], wg_ref[...]) + bg_ref[...])
    gate = gate * _silu(_dot(hn_ref[...], wz_ref[...]))
    o_ref[...] = (gj_ref[...].astype(_F32) * gate).astype(_BF16)


def _glu(g, hn, w_glu, b_glu, w_in, zb_col0, tm, tn):
    s, c = g.shape
    d = hn.shape[1]
    off = zb_col0 // tn
    return pl.pallas_call(
        _glu_kernel,
        grid=(c // tn, s // tm),
        in_specs=[pl.BlockSpec((tm, c), lambda j, i: (i, 0)),
                  pl.BlockSpec((tm, tn), lambda j, i: (i, j)),
                  pl.BlockSpec((tm, d), lambda j, i: (i, 0)),
                  _resident((c, tn), lambda j, i: (0, j)),
                  _resident((1, tn), lambda j, i: (0, j)),
                  _resident((d, tn), lambda j, i: (0, off + j))],
        out_specs=pl.BlockSpec((tm, tn), lambda j, i: (i, j)),
        out_shape=jax.ShapeDtypeStruct((s, c), _BF16),
        compiler_params=_params("parallel", "parallel"),
        name="branch_b_glu",
    )(g, g, hn, w_glu, b_glu, w_in)


def _merge_kernel(hn_ref, ya_ref, yb_ref, wga_ref, wgb_ref, wua_ref, wub_ref, pt_ref, o_ref):
    hn = hn_ref[...]
    m = _sigmoid(_dot(hn, wga_ref[...])) * _dot(ya_ref[...], wua_ref[...])
    m = m + _sigmoid(_dot(hn, wgb_ref[...])) * _dot(yb_ref[...], wub_ref[...])
    o_ref[...] = _dot(pt_ref[...], m.astype(_BF16)).astype(_BF16)


def _merge(hn, ya, yb, w_in, ga_col0, w_up_a, w_up_b, pt, tn):
    s, d = hn.shape
    ca, cb = ya.shape[1], yb.shape[1]
    nj = d // tn
    off_a = ga_col0 // tn
    off_b = off_a + nj
    return pl.pallas_call(
        _merge_kernel,
        grid=(nj, s // PERM_TILE),
        in_specs=[pl.BlockSpec((PERM_TILE, d), lambda j, i: (i, 0)),
                  pl.BlockSpec((PERM_TILE, ca), lambda j, i: (i, 0)),
                  pl.BlockSpec((PERM_TILE, cb), lambda j, i: (i, 0)),
                  _resident((d, tn), lambda j, i: (0, off_a + j)),
                  _resident((d, tn), lambda j, i: (0, off_b + j)),
                  _resident((ca, tn), lambda j, i: (0, j)),
                  _resident((cb, tn), lambda j, i: (0, j)),
                  _resident((PERM_TILE, PERM_TILE), lambda j, i: (0, 0))],
        out_specs=pl.BlockSpec((PERM_TILE, tn), lambda j, i: (i, j)),
        out_shape=jax.ShapeDtypeStruct((s, d), _BF16),
        compiler_params=_params("parallel", "parallel"),
        name="merge_unpermute",
    )(hn, ya, yb, w_in, w_in, w_up_a, w_up_b, pt)


def _out_kernel(m_ref, x_ref, w_ref, p_ref, wpp_ref, gp_ref, gf_ref, o_ref, hp_ref,
                *, n1, tn, final_norm):
    j = pl.program_id(1)
    nj = pl.num_programs(1)

    def rms(v, gain):
        ms = jnp.mean(v * v, axis=-1, keepdims=True)
        return v * lax.rsqrt(ms + EPS) * gain

    @pl.when(j < n1)
    def _():
        cols = pl.ds(pl.multiple_of(j * tn, tn), tn)
        o_ref[:, cols] = x_ref[...] + _dot(m_ref[...], w_ref[...])

    @pl.when(j == n1 - 1)
    def _():
        hp_ref[...] = rms(o_ref[...], gp_ref[...]).astype(_BF16)

    @pl.when(j >= n1)
    def _():
        cols = pl.ds(pl.multiple_of((j - n1) * tn, tn), tn)
        gate = _sigmoid(_dot(hp_ref[...], w_ref[...]))
        emb = _dot(p_ref[...].astype(_BF16), wpp_ref[...])
        o_ref[:, cols] = o_ref[:, cols] + gate * emb

    if final_norm:
        @pl.when(j == nj - 1)
        def _():
            o_ref[...] = rms(o_ref[...], gf_ref[...])


def _out(merged, x2, w_cat, p2, w_pp, g_ple, g_final, tm, tn, final_norm):
    s, d = x2.shape
    pd = p2.shape[1]
    n1 = d // tn
    first = lambda j: jnp.minimum(j, n1 - 1)
    second = lambda j: jnp.maximum(j - n1, 0)
    return pl.pallas_call(
        functools.partial(_out_kernel, n1=n1, tn=tn, final_norm=final_norm),
        grid=(s // tm, 2 * n1),
        in_specs=[pl.BlockSpec((tm, d), lambda i, j: (i, 0)),
                  pl.BlockSpec((tm, tn), lambda i, j: (i, first(j))),
                  pl.BlockSpec((d, tn), lambda i, j: (0, j)),
                  pl.BlockSpec((tm, pd), lambda i, j: (i, 0)),
                  pl.BlockSpec((pd, tn), lambda i, j: (0, second(j))),
                  pl.BlockSpec((1, d), lambda i, j: (0, 0)),
                  pl.BlockSpec((1, d), lambda i, j: (0, 0))],
        out_specs=pl.BlockSpec((tm, d), lambda i, j: (i, 0)),
        out_shape=jax.ShapeDtypeStruct((s, d), _F32),
        scratch_shapes=[pltpu.VMEM((tm, d), _BF16)],
        compiler_params=_params("parallel", "arbitrary"),
        name="out_proj_ple_norm",
    )(merged, x2, w_cat, p2, w_pp, g_ple, g_final)


def kernel(x, p, norm_in_g, w_in, conv_w, conv_b, lam_re, lam_im, log_dt, b_re, b_im, c_re, c_im,
           d_skip, w_glu, b_glu, w_up_conv, w_up_s5, w_out, ple_norm_g, w_ple_gate, w_ple_proj,
           final_norm_g):
    bsz, seq, d = x.shape
    depth = w_in.shape[0]
    conv_ch = conv_w.shape[-1]
    s5_ch = d_skip.shape[-1]
    assert bsz == 1 and seq % PERM_TILE == 0 and s5_ch % (HALVES * HALF_CH) == 0
    col_ub = 4 * conv_ch
    col_zb = col_ub + s5_ch
    col_ga = col_zb + s5_ch
    tm_big = 2 * PERM_TILE if seq % (2 * PERM_TILE) == 0 else PERM_TILE
    tn = min(512, conv_ch, s5_ch)
    tn_wide = min(1024, d)

    pm_np = _perm_matrix()
    pm = jnp.asarray(pm_np, _BF16)
    pt = jnp.asarray(pm_np.T, _BF16)

    h = x.reshape(seq, d)
    for i in range(depth):
        last = i == depth - 1
        w_in_b = w_in[i].astype(_BF16)
        w_cat = jnp.concatenate([w_out[i], w_ple_gate[i]], axis=1).astype(_BF16)
        wb, wc, lam_l, p64 = _s5_params(lam_re[i], lam_im[i], log_dt[i], b_re[i], b_im[i],
                                        c_re[i], c_im[i])
        hn = _prep(h, norm_in_g[i][None, :], pm)
        ya = _branch_a(hn, w_in_b, conv_w[i], conv_b[i][None, :], conv_ch, tm_big, tn)
        g = _s5(hn, w_in_b, col_ub, wb, wc, lam_l, p64, d_skip[i][None, :], s5_ch)
        yb = _glu(g, hn, w_glu[i].astype(_BF16), b_glu[i][None, :], w_in_b, col_zb, tm_big, tn)
        merged = _merge(hn, ya, yb, w_in_b, col_ga, w_up_conv[i].astype(_BF16),
                        w_up_s5[i].astype(_BF16), pt, tn_wide)
        h = _out(merged, h, w_cat, p[i].reshape(seq, -1), w_ple_proj[i].astype(_BF16),
                 ple_norm_g[i][None, :], final_norm_g[None, :], PERM_TILE, tn_wide, last)
    return h.reshape(bsz, seq, d)
```

```python
import functools

import numpy as np
import jax
import jax.numpy as jnp
from jax import lax
from jax.experimental import pallas as pl
from jax.experimental.pallas import tpu as pltpu

EPS = 1e-6
LANES = 128
SUBLANES = 8
BF16_ROWS = 16
SCAN_CHUNK = 64
PERM_TILE = SUBLANES * SCAN_CHUNK
S5_GROUP_CH = 16
S5_STATE = 64
GROUPS_PER_HALF = 16
HALF_CH = GROUPS_PER_HALF * S5_GROUP_CH
HALF_ST = GROUPS_PER_HALF * S5_STATE
HALVES = 2
SCAN_LANES = 512
VMEM_LIMIT = 60 * 1024 * 1024

_BF16 = jnp.bfloat16
_F32 = jnp.float32


def _dot(a, b):
    return jnp.dot(a, b, preferred_element_type=_F32)


def _sigmoid(v):
    return jax.nn.sigmoid(v)


def _silu(v):
    return v * jax.nn.sigmoid(v)


def _perm_matrix():
    pm = np.zeros((PERM_TILE, PERM_TILE), np.float32)
    for j in range(SUBLANES):
        for l in range(SCAN_CHUNK):
            pm[l * SUBLANES + j, j * SCAN_CHUNK + l] = 1.0
    return pm


def _params(*sem):
    return pltpu.CompilerParams(dimension_semantics=sem, vmem_limit_bytes=VMEM_LIMIT)


def _resident(shape, index_map):
    return pl.BlockSpec(shape, index_map, pipeline_mode=pl.Buffered(1))


def _cast_specs(jobs, steps, inner):
    in_specs, out_specs, out_shapes = [], [], []
    slab = lambda a, b: (a * inner + b, 0)
    for job in jobs:
        rows = job[0].shape[0]
        assert rows % (steps * BF16_ROWS) == 0 and all(w.shape[0] == rows for w in job)
        cols = sum(w.shape[1] for w in job)
        in_specs += [pl.BlockSpec((rows // steps, w.shape[1]), slab) for w in job]
        out_specs.append(pl.BlockSpec((rows // steps, cols), slab))
        out_shapes.append(jax.ShapeDtypeStruct((rows, cols), _BF16))
    return in_specs, out_specs, out_shapes


def _cast_all(src_refs, dst_refs):
    src = iter(src_refs)
    for dst in dst_refs:
        col = 0
        while col < dst.shape[1]:
            w = next(src)
            dst[:, col:col + w.shape[1]] = w[...].astype(_BF16)
            col += w.shape[1]


def _prep_kernel(x_ref, g_ref, pm_ref, o_ref):
    x = x_ref[...]
    ms = jnp.mean(x * x, axis=-1, keepdims=True)
    hn = (x * lax.rsqrt(ms + EPS) * g_ref[...]).astype(_BF16)
    o_ref[...] = _dot(pm_ref[...], hn).astype(_BF16)


def _prep(x2, gain, pm):
    s, d = x2.shape
    return pl.pallas_call(
        _prep_kernel,
        grid=(s // PERM_TILE,),
        in_specs=[pl.BlockSpec((PERM_TILE, d), lambda i: (i, 0)),
                  pl.BlockSpec((1, d), lambda i: (0, 0)),
                  pl.BlockSpec((PERM_TILE, PERM_TILE), lambda i: (0, 0))],
        out_specs=pl.BlockSpec((PERM_TILE, d), lambda i: (i, 0)),
        out_shape=jax.ShapeDtypeStruct((s, d), _BF16),
        compiler_params=_params("parallel"),
        name="prep_norm_permute",
    )(x2, gain, pm)


def _branch_a_kernel(*refs, n_src, n_dst):
    hn_ref, wx_ref, wc_ref, wb_ref, wz_ref, cw_ref, cb_ref = refs[:7]
    cast_src = refs[7:7 + n_src]
    o_ref = refs[7 + n_src]
    cast_dst = refs[8 + n_src:8 + n_src + n_dst]
    v_ref, tail_ref = refs[8 + n_src + n_dst:]
    i = pl.program_id(1)

    @pl.when(i == 0)
    def _():
        tail_ref[...] = jnp.zeros_like(tail_ref)

    _cast_all(cast_src, cast_dst)
    hn = hn_ref[...]
    v_ref[...] = _dot(hn, wc_ref[...]) * _dot(hn, wx_ref[...])
    gate = _dot(hn, wb_ref[...]) * _silu(_dot(hn, wz_ref[...]))
    tm, tn = v_ref.shape
    sub = lax.broadcasted_iota(jnp.int32, (SUBLANES, tn), 0)
    cw = cw_ref[...]

    def from_previous_chunk(cur, prev_last):
        return jnp.where(sub == 0, prev_last, pltpu.roll(cur, 1, 0))

    for t in range(tm // PERM_TILE):
        r0 = t * PERM_TILE
        v = v_ref[r0:r0 + PERM_TILE, :]
        tail = v[PERM_TILE - 2 * SUBLANES:, :]
        prev = tail_ref[...]
        r62 = from_previous_chunk(tail[:SUBLANES], prev[SUBLANES - 1:SUBLANES])
        r63 = from_previous_chunk(tail[SUBLANES:], prev[2 * SUBLANES - 1:])
        tail_ref[...] = tail
        vs1 = jnp.concatenate([r63, v[:PERM_TILE - SUBLANES]], axis=0)
        vs2 = jnp.concatenate([r62, r63, v[:PERM_TILE - 2 * SUBLANES]], axis=0)
        conv = cw[0:1] * vs2 + cw[1:2] * vs1 + cw[2:3] * v + cb_ref[...]
        o_ref[r0:r0 + PERM_TILE, :] = (gate[r0:r0 + PERM_TILE, :] * conv).astype(_BF16)


def _branch_a(hn, w_in, conv_w, conv_b, conv_ch, tm, tn, cast_jobs):
    s, d = hn.shape
    nj, ni = conv_ch // tn, s // tm
    wspec = lambda k: _resident((d, tn), lambda j, i, k=k: (0, k * nj + j))
    cast_in, cast_out, cast_shapes = _cast_specs(cast_jobs, nj * ni, ni)
    outs = pl.pallas_call(
        functools.partial(_branch_a_kernel, n_src=len(cast_in), n_dst=len(cast_out)),
        grid=(nj, ni),
        in_specs=[pl.BlockSpec((tm, d), lambda j, i: (i, 0)),
                  wspec(0), wspec(1), wspec(2), wspec(3),
                  _resident((conv_w.shape[0], tn), lambda j, i: (0, j)),
                  _resident((1, tn), lambda j, i: (0, j))] + cast_in,
        out_specs=[pl.BlockSpec((tm, tn), lambda j, i: (i, j))] + cast_out,
        out_shape=[jax.ShapeDtypeStruct((s, conv_ch), _BF16)] + cast_shapes,
        scratch_shapes=[pltpu.VMEM((tm, tn), _F32),
                        pltpu.VMEM((2 * SUBLANES, tn), _F32)],
        compiler_params=_params("arbitrary", "arbitrary"),
        name="branch_a_conv",
    )(hn, w_in, w_in, w_in, w_in, conv_w, conv_b, *[w for job in cast_jobs for w in job])
    return outs[0], outs[1:]


def _s5_kernel(*refs, n_src, n_dst):
    hn_ref, wu_ref, wb_ref, wc_ref, lam_ref, p64_ref, d_ref = refs[:7]
    cast_src = refs[7:7 + n_src]
    o_ref = refs[7 + n_src]
    cast_dst = refs[8 + n_src:8 + n_src + n_dst]
    bu_ref, hb_ref, st_ref = refs[8 + n_src + n_dst:]
    i = pl.program_id(1)
    w = HALF_ST
    shape = (SUBLANES, SCAN_LANES)

    @pl.when(i == 0)
    def _():
        st_ref[...] = jnp.zeros_like(st_ref)

    _cast_all(cast_src, cast_dst)
    u = _dot(hn_ref[...], wu_ref[...])
    ub = u.astype(_BF16)
    for h in range(HALVES):
        bu_ref[h] = _dot(ub[:, h * HALF_CH:(h + 1) * HALF_CH], wb_ref[h])

    for h in range(HALVES):
        for s in range(w // SCAN_LANES):
            re = pl.ds(s * SCAN_LANES, SCAN_LANES)
            im = pl.ds(w + s * SCAN_LANES, SCAN_LANES)
            lr = jnp.broadcast_to(lam_ref[h, :, re], shape)
            li = jnp.broadcast_to(lam_ref[h, :, im], shape)

            def advance(hr, hi, rows):
                return (lr * hr - li * hi + bu_ref[h, rows, re],
                        lr * hi + li * hr + bu_ref[h, rows, im])

            def end_step(l, carry):
                rows = pl.ds(pl.multiple_of(l * SUBLANES, SUBLANES), SUBLANES)
                return advance(*carry, rows)

            zero = jnp.zeros(shape, _F32)
            er, ei = lax.fori_loop(0, SCAN_CHUNK, end_step, (zero, zero), unroll=8)

            pr = p64_ref[h, :, re]
            pi = p64_ref[h, :, im]
            cr = st_ref[h, 0:1, re]
            ci = st_ref[h, 0:1, im]
            rows_r, rows_i = [], []
            for j in range(SUBLANES):
                rows_r.append(cr)
                rows_i.append(ci)
                cr, ci = (er[j:j + 1] + (pr * cr - pi * ci),
                          ei[j:j + 1] + (pr * ci + pi * cr))
            st_ref[h, 0:1, re] = cr
            st_ref[h, 0:1, im] = ci
            cin = (jnp.concatenate(rows_r, axis=0), jnp.concatenate(rows_i, axis=0))

            def scan_step(l2, carry):
                r0 = pl.multiple_of(l2 * 2 * SUBLANES, 2 * SUBLANES)
                ar, ai = advance(*carry, pl.ds(r0, SUBLANES))
                br, bi = advance(ar, ai, pl.ds(r0 + SUBLANES, SUBLANES))
                rows = pl.ds(r0, 2 * SUBLANES)
                hb_ref[h, rows, re] = jnp.concatenate([ar, br], axis=0).astype(_BF16)
                hb_ref[h, rows, im] = jnp.concatenate([ai, bi], axis=0).astype(_BF16)
                return br, bi

            lax.fori_loop(0, SCAN_CHUNK // 2, scan_step, cin, unroll=4)

    y = jnp.concatenate([_dot(hb_ref[h], wc_ref[h]) for h in range(HALVES)], axis=1)
    o_ref[...] = jax.nn.gelu(y + d_ref[...] * u).astype(_BF16)


def _s5(hn, w_ub, wb, wc, lam, p64, d_skip, cast_jobs):
    s, d = hn.shape
    s5_ch = w_ub.shape[1]
    bc = HALVES * HALF_CH
    nb, ni = s5_ch // bc, s // PERM_TILE
    cast_in, cast_out, cast_shapes = _cast_specs(cast_jobs, nb * ni, ni)
    outs = pl.pallas_call(
        functools.partial(_s5_kernel, n_src=len(cast_in), n_dst=len(cast_out)),
        grid=(nb, ni),
        in_specs=[pl.BlockSpec((PERM_TILE, d), lambda b, i: (i, 0)),
                  _resident((d, bc), lambda b, i: (0, b)),
                  _resident((HALVES, HALF_CH, 2 * HALF_ST), lambda b, i: (b, 0, 0)),
                  _resident((HALVES, 2 * HALF_ST, HALF_CH), lambda b, i: (b, 0, 0)),
                  _resident((HALVES, 1, 2 * HALF_ST), lambda b, i: (b, 0, 0)),
                  _resident((HALVES, 1, 2 * HALF_ST), lambda b, i: (b, 0, 0)),
                  _resident((1, bc), lambda b, i: (0, b))] + cast_in,
        out_specs=[pl.BlockSpec((PERM_TILE, bc), lambda b, i: (i, b))] + cast_out,
        out_shape=[jax.ShapeDtypeStruct((s, s5_ch), _BF16)] + cast_shapes,
        scratch_shapes=[pltpu.VMEM((HALVES, PERM_TILE, 2 * HALF_ST), _F32),
                        pltpu.VMEM((HALVES, PERM_TILE, 2 * HALF_ST), _BF16),
                        pltpu.VMEM((HALVES, SUBLANES, 2 * HALF_ST), _F32)],
        compiler_params=_params("arbitrary", "arbitrary"),
        name="branch_b_s5",
    )(hn, w_ub, wb, wc, lam, p64, d_skip, *[w for job in cast_jobs for w in job])
    return outs[0], outs[1:]


def _s5_params(lam_re, lam_im, log_dt, b_re, b_im, c_re, c_im):
    g = lam_re.shape[0]
    nh = g // GROUPS_PER_HALF
    dt = jnp.exp(log_dt)[:, None]
    a, b = lam_re * dt, lam_im * dt
    mag = jnp.exp(a)
    lbr, lbi = mag * jnp.cos(b), mag * jnp.sin(b)
    den = lam_re * lam_re + lam_im * lam_im
    qr = ((lbr - 1.0) * lam_re + lbi * lam_im) / den
    qi = (lbi * lam_re - (lbr - 1.0) * lam_im) / den
    bbr = qr[..., None] * b_re - qi[..., None] * b_im
    bbi = qr[..., None] * b_im + qi[..., None] * b_re
    mag64 = jnp.exp(SCAN_CHUNK * a)
    p64r, p64i = mag64 * jnp.cos(SCAN_CHUNK * b), mag64 * jnp.sin(SCAN_CHUNK * b)

    def block_diag(m, rows_per_group, cols_per_group):
        t = jnp.tile(m, (1, 1, GROUPS_PER_HALF))
        r = lax.broadcasted_iota(jnp.int32, t.shape, 1) // rows_per_group
        c = lax.broadcasted_iota(jnp.int32, t.shape, 2) // cols_per_group
        return jnp.where(r == c, t, 0.0)

    def in_mat(m):
        m = m.reshape(nh, GROUPS_PER_HALF, S5_STATE, S5_GROUP_CH).transpose(0, 1, 3, 2)
        return block_diag(m.reshape(nh, HALF_CH, S5_STATE), S5_GROUP_CH, S5_STATE)

    def out_mat(m):
        m = m.reshape(nh, GROUPS_PER_HALF, S5_GROUP_CH, S5_STATE).transpose(0, 1, 3, 2)
        return block_diag(m.reshape(nh, HALF_ST, S5_GROUP_CH), S5_STATE, S5_GROUP_CH)

    wb = jnp.concatenate([in_mat(bbr), in_mat(bbi)], axis=2).astype(_BF16)
    wc = jnp.concatenate([out_mat(c_re), out_mat(-c_im)], axis=1).astype(_BF16)
    lanes = lambda r, m: jnp.concatenate([r.reshape(nh, 1, HALF_ST), m.reshape(nh, 1, HALF_ST)], axis=2)
    return wb, wc, lanes(lbr, lbi), lanes(p64r, p64i)


def _glu_kernel(g_ref, gj_ref, hn_ref, wg_ref, bg_ref, wz_ref, o_ref):
    gate = _sigmoid(_dot(g_ref[...], wg_ref[...]) + bg_ref[...])
    gate = gate * _silu(_dot(hn_ref[...], wz_ref[...]))
    o_ref[...] = (gj_ref[...].astype(_F32) * gate).astype(_BF16)


def _glu(g, hn, w_glu, b_glu, w_in, zb_col0, tm, tn):
    s, c = g.shape
    d = hn.shape[1]
    off = zb_col0 // tn
    return pl.pallas_call(
        _glu_kernel,
        grid=(c // tn, s // tm),
        in_specs=[pl.BlockSpec((tm, c), lambda j, i: (i, 0)),
                  pl.BlockSpec((tm, tn), lambda j, i: (i, j)),
                  pl.BlockSpec((tm, d), lambda j, i: (i, 0)),
                  _resident((c, tn), lambda j, i: (0, j)),
                  _resident((1, tn), lambda j, i: (0, j)),
                  _resident((d, tn), lambda j, i: (0, off + j))],
        out_specs=pl.BlockSpec((tm, tn), lambda j, i: (i, j)),
        out_shape=jax.ShapeDtypeStruct((s, c), _BF16),
        compiler_params=_params("parallel", "parallel"),
        name="branch_b_glu",
    )(g, g, hn, w_glu, b_glu, w_in)


def _merge_kernel(hn_ref, ya_ref, yb_ref, wga_ref, wgb_ref, wua_ref, wub_ref, o_ref, m_ref):
    hn = hn_ref[...]
    m = _sigmoid(_dot(hn, wga_ref[...])) * _dot(ya_ref[...], wua_ref[...])
    m = m + _sigmoid(_dot(hn, wgb_ref[...])) * _dot(yb_ref[...], wub_ref[...])
    n_slabs = m_ref.shape[0]
    for k in range(n_slabs):
        m_ref[k] = m[:, k * LANES:(k + 1) * LANES]
    for j in range(SUBLANES):
        for k in range(n_slabs):
            rows = m_ref[k, pl.ds(j, SCAN_CHUNK, stride=SUBLANES), :]
            o_ref[j * SCAN_CHUNK:(j + 1) * SCAN_CHUNK, k * LANES:(k + 1) * LANES] = rows.astype(_BF16)


def _merge(hn, ya, yb, w_in, ga_col0, w_up_a, w_up_b, tn):
    s, d = hn.shape
    ca, cb = ya.shape[1], yb.shape[1]
    nj = d // tn
    off_a = ga_col0 // tn
    off_b = off_a + nj
    return pl.pallas_call(
        _merge_kernel,
        grid=(nj, s // PERM_TILE),
        in_specs=[pl.BlockSpec((PERM_TILE, d), lambda j, i: (i, 0)),
                  pl.BlockSpec((PERM_TILE, ca), lambda j, i: (i, 0)),
                  pl.BlockSpec((PERM_TILE, cb), lambda j, i: (i, 0)),
                  _resident((d, tn), lambda j, i: (0, off_a + j)),
                  _resident((d, tn), lambda j, i: (0, off_b + j)),
                  _resident((ca, tn), lambda j, i: (0, j)),
                  _resident((cb, tn), lambda j, i: (0, j))],
        out_specs=pl.BlockSpec((PERM_TILE, tn), lambda j, i: (i, j)),
        out_shape=jax.ShapeDtypeStruct((s, d), _BF16),
        scratch_shapes=[pltpu.VMEM((tn // LANES, PERM_TILE, LANES), _F32)],
        compiler_params=_params("parallel", "parallel"),
        name="merge_unpermute",
    )(hn, ya, yb, w_in, w_in, w_up_a, w_up_b)


def _out_kernel(m_ref, x_ref, w_ref, p_ref, wpp_ref, gp_ref, gf_ref, o_ref,
                hp_ref, ss1_ref, ss2_ref, *, n1, tn, final_norm):
    j = pl.program_id(1)
    nj = pl.num_programs(1)
    d = o_ref.shape[1]

    @pl.when(j == 0)
    def _():
        ss1_ref[...] = jnp.zeros_like(ss1_ref)
        ss2_ref[...] = jnp.zeros_like(ss2_ref)

    @pl.when(j < n1)
    def _():
        cols = pl.ds(pl.multiple_of(j * tn, tn), tn)
        h1 = x_ref[...] + _dot(m_ref[...], w_ref[...])
        o_ref[:, cols] = h1
        hp_ref[:, cols] = (h1 * gp_ref[:, cols]).astype(_BF16)
        ss1_ref[...] += jnp.sum(h1 * h1, axis=-1, keepdims=True)

    @pl.when(j >= n1)
    def _():
        cols = pl.ds(pl.multiple_of((j - n1) * tn, tn), tn)
        rinv = lax.rsqrt(ss1_ref[...] * (1.0 / d) + EPS)
        gate = _sigmoid(rinv * _dot(hp_ref[...], w_ref[...]))
        emb = _dot(p_ref[...].astype(_BF16), wpp_ref[...])
        h2 = o_ref[:, cols] + gate * emb
        o_ref[:, cols] = h2
        ss2_ref[...] += jnp.sum(h2 * h2, axis=-1, keepdims=True)

    if final_norm:
        @pl.when(j == nj - 1)
        def _():
            rinv = lax.rsqrt(ss2_ref[...] * (1.0 / d) + EPS)
            o_ref[...] = o_ref[...] * rinv * gf_ref[...]


def _out(merged, x2, w_cat, p2, w_pp, g_ple, g_final, tm, tn, final_norm):
    s, d = x2.shape
    pd = p2.shape[1]
    n1 = d // tn
    first = lambda j: jnp.minimum(j, n1 - 1)
    second = lambda j: jnp.maximum(j - n1, 0)
    return pl.pallas_call(
        functools.partial(_out_kernel, n1=n1, tn=tn, final_norm=final_norm),
        grid=(s // tm, 2 * n1),
        in_specs=[pl.BlockSpec((tm, d), lambda i, j: (i, 0)),
                  pl.BlockSpec((tm, tn), lambda i, j: (i, first(j))),
                  pl.BlockSpec((d, tn), lambda i, j: (0, j)),
                  pl.BlockSpec((tm, pd), lambda i, j: (i, 0)),
                  pl.BlockSpec((pd, tn), lambda i, j: (0, second(j))),
                  pl.BlockSpec((1, d), lambda i, j: (0, 0)),
                  pl.BlockSpec((1, d), lambda i, j: (0, 0))],
        out_specs=pl.BlockSpec((tm, d), lambda i, j: (i, 0)),
        out_shape=jax.ShapeDtypeStruct((s, d), _F32),
        scratch_shapes=[pltpu.VMEM((tm, d), _BF16),
                        pltpu.VMEM((tm, 1), _F32),
                        pltpu.VMEM((tm, 1), _F32)],
        compiler_params=_params("parallel", "arbitrary"),
        name="out_proj_ple_norm",
    )(merged, x2, w_cat, p2, w_pp, g_ple, g_final)


def kernel(x, p, norm_in_g, w_in, conv_w, conv_b, lam_re, lam_im, log_dt, b_re, b_im, c_re, c_im,
           d_skip, w_glu, b_glu, w_up_conv, w_up_s5, w_out, ple_norm_g, w_ple_gate, w_ple_proj,
           final_norm_g):
    bsz, seq, d = x.shape
    depth = w_in.shape[0]
    conv_ch = conv_w.shape[-1]
    s5_ch = d_skip.shape[-1]
    assert bsz == 1 and seq % PERM_TILE == 0 and s5_ch % (HALVES * HALF_CH) == 0
    col_ub = 4 * conv_ch
    col_zb = col_ub + s5_ch
    col_ga = col_zb + s5_ch
    tm_big = 2 * PERM_TILE if seq % (2 * PERM_TILE) == 0 else PERM_TILE
    tn = min(512, conv_ch, s5_ch)
    tn_wide = min(1024, d)

    pm = jnp.asarray(_perm_matrix(), _BF16)

    h = x.reshape(seq, d)
    for i in range(depth):
        last = i == depth - 1
        wb, wc, lam_l, p64 = _s5_params(lam_re[i], lam_im[i], log_dt[i], b_re[i], b_im[i],
                                        c_re[i], c_im[i])
        hn = _prep(h, norm_in_g[i][None, :], pm)

        w_ub = w_in[i][:, col_ub:col_zb].astype(_BF16)
        g, (w_in_b,) = _s5(hn, w_ub, wb, wc, lam_l, p64, d_skip[i][None, :], [(w_in[i],)])
        ya, (w_glu_b, w_ua_b, w_us_b, w_cat) = _branch_a(
            hn, w_in_b, conv_w[i], conv_b[i][None, :], conv_ch, tm_big, tn,
            [(w_glu[i],), (w_up_conv[i],), (w_up_s5[i],), (w_out[i], w_ple_gate[i])])

        yb = _glu(g, hn, w_glu_b, b_glu[i][None, :], w_in_b, col_zb, tm_big, tn)
        merged = _merge(hn, ya, yb, w_in_b, col_ga, w_ua_b, w_us_b, tn_wide)
        h = _out(merged, h, w_cat, p[i].reshape(seq, -1), w_ple_proj[i].astype(_BF16),
                 ple_norm_g[i][None, :], final_norm_g[None, :], PERM_TILE, tn_wide, last)
    return h.reshape(bsz, seq, d)
```

```python
import functools

import numpy as np
import jax
import jax.numpy as jnp
from jax import lax
from jax.experimental import pallas as pl
from jax.experimental.pallas import tpu as pltpu

EPS = 1e-6
LANES = 128
SUBLANES = 8
BF16_ROWS = 16
SCAN_CHUNK = 64
PERM_TILE = SUBLANES * SCAN_CHUNK
S5_GROUP_CH = 16
S5_STATE = 64
GROUPS_PER_HALF = 16
HALF_CH = GROUPS_PER_HALF * S5_GROUP_CH
HALF_ST = GROUPS_PER_HALF * S5_STATE
HALVES = 2
SCAN_LANES = 512
VMEM_LIMIT = 60 * 1024 * 1024

_BF16 = jnp.bfloat16
_F32 = jnp.float32


def _dot(a, b):
    return jnp.dot(a, b, preferred_element_type=_F32)


def _sigmoid(v):
    return jax.nn.sigmoid(v)


def _silu(v):
    return v * jax.nn.sigmoid(v)


def _perm_matrix():
    pm = np.zeros((PERM_TILE, PERM_TILE), np.float32)
    for j in range(SUBLANES):
        for l in range(SCAN_CHUNK):
            pm[l * SUBLANES + j, j * SCAN_CHUNK + l] = 1.0
    return pm


def _params(*sem):
    return pltpu.CompilerParams(dimension_semantics=sem, vmem_limit_bytes=VMEM_LIMIT)


def _resident(shape, index_map):
    return pl.BlockSpec(shape, index_map, pipeline_mode=pl.Buffered(1))


def _cast_specs(jobs, steps, slab_index):
    in_specs, out_specs, out_shapes = [], [], []
    slab = lambda *idx: (slab_index(*idx), 0)
    for job in jobs:
        rows = job[0].shape[0]
        assert rows % (steps * BF16_ROWS) == 0 and all(w.shape[0] == rows for w in job)
        cols = sum(w.shape[1] for w in job)
        in_specs += [pl.BlockSpec((rows // steps, w.shape[1]), slab) for w in job]
        out_specs.append(pl.BlockSpec((rows // steps, cols), slab))
        out_shapes.append(jax.ShapeDtypeStruct((rows, cols), _BF16))
    return in_specs, out_specs, out_shapes


def _cast_all(src_refs, dst_refs):
    src = iter(src_refs)
    for dst in dst_refs:
        col = 0
        while col < dst.shape[1]:
            w = next(src)
            dst[:, col:col + w.shape[1]] = w[...].astype(_BF16)
            col += w.shape[1]


def _prep_kernel(x_ref, g_ref, pm_ref, o_ref):
    x = x_ref[...]
    ms = jnp.mean(x * x, axis=-1, keepdims=True)
    hn = (x * lax.rsqrt(ms + EPS) * g_ref[...]).astype(_BF16)
    o_ref[...] = _dot(pm_ref[...], hn).astype(_BF16)


def _prep(x2, gain, pm):
    s, d = x2.shape
    return pl.pallas_call(
        _prep_kernel,
        grid=(s // PERM_TILE,),
        in_specs=[pl.BlockSpec((PERM_TILE, d), lambda i: (i, 0)),
                  pl.BlockSpec((1, d), lambda i: (0, 0)),
                  pl.BlockSpec((PERM_TILE, PERM_TILE), lambda i: (0, 0))],
        out_specs=pl.BlockSpec((PERM_TILE, d), lambda i: (i, 0)),
        out_shape=jax.ShapeDtypeStruct((s, d), _BF16),
        compiler_params=_params("parallel"),
        name="prep_norm_permute",
    )(x2, gain, pm)


def _branch_a_kernel(*refs, n_src, n_dst):
    hn_ref, wx_ref, wc_ref, wb_ref, wz_ref, cw_ref, cb_ref = refs[:7]
    cast_src = refs[7:7 + n_src]
    o_ref = refs[7 + n_src]
    cast_dst = refs[8 + n_src:8 + n_src + n_dst]
    v_ref, tail_ref = refs[8 + n_src + n_dst:]
    i = pl.program_id(1)

    @pl.when(i == 0)
    def _():
        tail_ref[...] = jnp.zeros_like(tail_ref)

    _cast_all(cast_src, cast_dst)
    hn = hn_ref[...]
    v_ref[...] = _dot(hn, wc_ref[...]) * _dot(hn, wx_ref[...])
    gate = _dot(hn, wb_ref[...]) * _silu(_dot(hn, wz_ref[...]))
    tm, tn = v_ref.shape
    sub = lax.broadcasted_iota(jnp.int32, (SUBLANES, tn), 0)
    cw = cw_ref[...]

    def from_previous_chunk(cur, prev_last):
        return jnp.where(sub == 0, prev_last, pltpu.roll(cur, 1, 0))

    for t in range(tm // PERM_TILE):
        r0 = t * PERM_TILE
        v = v_ref[r0:r0 + PERM_TILE, :]
        tail = v[PERM_TILE - 2 * SUBLANES:, :]
        prev = tail_ref[...]
        r62 = from_previous_chunk(tail[:SUBLANES], prev[SUBLANES - 1:SUBLANES])
        r63 = from_previous_chunk(tail[SUBLANES:], prev[2 * SUBLANES - 1:])
        tail_ref[...] = tail
        vs1 = jnp.concatenate([r63, v[:PERM_TILE - SUBLANES]], axis=0)
        vs2 = jnp.concatenate([r62, r63, v[:PERM_TILE - 2 * SUBLANES]], axis=0)
        conv = cw[0:1] * vs2 + cw[1:2] * vs1 + cw[2:3] * v + cb_ref[...]
        o_ref[r0:r0 + PERM_TILE, :] = (gate[r0:r0 + PERM_TILE, :] * conv).astype(_BF16)


def _branch_a(hn, w_in, conv_w, conv_b, conv_ch, tm, tn, cast_jobs):
    s, d = hn.shape
    nj, ni = conv_ch // tn, s // tm
    wspec = lambda k: _resident((d, tn), lambda j, i, k=k: (0, k * nj + j))
    cast_in, cast_out, cast_shapes = _cast_specs(cast_jobs, nj * ni, lambda j, i: j * ni + i)
    outs = pl.pallas_call(
        functools.partial(_branch_a_kernel, n_src=len(cast_in), n_dst=len(cast_out)),
        grid=(nj, ni),
        in_specs=[pl.BlockSpec((tm, d), lambda j, i: (i, 0)),
                  wspec(0), wspec(1), wspec(2), wspec(3),
                  _resident((conv_w.shape[0], tn), lambda j, i: (0, j)),
                  _resident((1, tn), lambda j, i: (0, j))] + cast_in,
        out_specs=[pl.BlockSpec((tm, tn), lambda j, i: (i, j))] + cast_out,
        out_shape=[jax.ShapeDtypeStruct((s, conv_ch), _BF16)] + cast_shapes,
        scratch_shapes=[pltpu.VMEM((tm, tn), _F32),
                        pltpu.VMEM((2 * SUBLANES, tn), _F32)],
        compiler_params=_params("arbitrary", "arbitrary"),
        name="branch_a_conv",
    )(hn, w_in, w_in, w_in, w_in, conv_w, conv_b, *[w for job in cast_jobs for w in job])
    return outs[0], outs[1:]


def _s5_kernel(*refs, n_src, n_dst, tiles_per_block):
    hn_ref, wu_ref, wb_ref, wc_ref, lam_ref, p64_ref, d_ref = refs[:7]
    cast_src = refs[7:7 + n_src]
    o_ref = refs[7 + n_src]
    cast_dst = refs[8 + n_src:8 + n_src + n_dst]
    bu_ref, hb_ref, un_ref, uo_ref, st_ref, tok_ref = refs[8 + n_src + n_dst:]
    s = pl.program_id(0)
    w = HALF_ST
    shape = (SUBLANES, SCAN_LANES)
    tok_ref[...] = jnp.zeros_like(tok_ref)

    @pl.when(s == 0)
    def _():
        bu_ref[...] = jnp.zeros_like(bu_ref)
        uo_ref[...] = jnp.zeros_like(uo_ref)

    @pl.when(jnp.maximum(s - 1, 0) % tiles_per_block == 0)
    def _():
        st_ref[...] = jnp.zeros_like(st_ref)

    _cast_all(cast_src, cast_dst)
    un_ref[...] = _dot(hn_ref[...], wu_ref[...])

    for h in range(HALVES):
        for g in range(w // SCAN_LANES):
            re = slice(g * SCAN_LANES, (g + 1) * SCAN_LANES)
            im = slice(w + g * SCAN_LANES, w + (g + 1) * SCAN_LANES)
            lr = jnp.broadcast_to(lam_ref[h, :, re], shape)
            li = jnp.broadcast_to(lam_ref[h, :, im], shape)

            def advance(hr, hi, l):
                rows = slice(l * SUBLANES, (l + 1) * SUBLANES)
                return (lr * hr - li * hi + bu_ref[h, rows, re],
                        lr * hi + li * hr + bu_ref[h, rows, im])

            er = ei = jnp.where(s < 0, tok_ref[...], 0.0)
            for l in range(SCAN_CHUNK):
                er, ei = advance(er, ei, l)

            pr = p64_ref[h, :, re]
            pi = p64_ref[h, :, im]
            cr = st_ref[h, 0:1, re]
            ci = st_ref[h, 0:1, im]
            rows_r, rows_i = [], []
            for j in range(SUBLANES):
                rows_r.append(cr)
                rows_i.append(ci)
                cr, ci = (er[j:j + 1] + (pr * cr - pi * ci),
                          ei[j:j + 1] + (pr * ci + pi * cr))
            st_ref[h, 0:1, re] = cr
            st_ref[h, 0:1, im] = ci
            hr = jnp.concatenate(rows_r, axis=0)
            hi = jnp.concatenate(rows_i, axis=0)

            for l in range(0, SCAN_CHUNK, 2):
                ar, ai = advance(hr, hi, l)
                hr, hi = advance(ar, ai, l + 1)
                rows = slice(l * SUBLANES, (l + 2) * SUBLANES)
                hb_ref[h, rows, re] = jnp.concatenate([ar, hr], axis=0).astype(_BF16)
                hb_ref[h, rows, im] = jnp.concatenate([ai, hi], axis=0).astype(_BF16)
            tok_ref[...] = hr

    y = jnp.concatenate([_dot(hb_ref[h], wc_ref[h]) for h in range(HALVES)], axis=1)
    o_ref[...] = jax.nn.gelu(y + d_ref[...] * uo_ref[...]).astype(_BF16)
    u = un_ref[...]
    uo_ref[...] = u
    ub = u.astype(_BF16)
    for h in range(HALVES):
        bu_ref[h] = _dot(ub[:, h * HALF_CH:(h + 1) * HALF_CH], wb_ref[h])


def _s5(hn, w_ub, wb, wc, lam, p64, d_skip, cast_jobs):
    s, d = hn.shape
    s5_ch = w_ub.shape[1]
    bc = HALVES * HALF_CH
    nb, ni = s5_ch // bc, s // PERM_TILE
    tiles = nb * ni
    cur = lambda t: jnp.minimum(t, tiles - 1)
    prev = lambda t: jnp.maximum(t - 1, 0)
    cast_in, cast_out, cast_shapes = _cast_specs(cast_jobs, tiles, cur)
    outs = pl.pallas_call(
        functools.partial(_s5_kernel, n_src=len(cast_in), n_dst=len(cast_out),
                          tiles_per_block=ni),
        grid=(tiles + 1,),
        in_specs=[pl.BlockSpec((PERM_TILE, d), lambda t: (cur(t) % ni, 0)),
                  _resident((d, bc), lambda t: (0, cur(t) // ni)),
                  _resident((HALVES, HALF_CH, 2 * HALF_ST), lambda t: (cur(t) // ni, 0, 0)),
                  _resident((HALVES, 2 * HALF_ST, HALF_CH), lambda t: (prev(t) // ni, 0, 0)),
                  _resident((HALVES, 1, 2 * HALF_ST), lambda t: (prev(t) // ni, 0, 0)),
                  _resident((HALVES, 1, 2 * HALF_ST), lambda t: (prev(t) // ni, 0, 0)),
                  _resident((1, bc), lambda t: (0, prev(t) // ni))] + cast_in,
        out_specs=[pl.BlockSpec((PERM_TILE, bc),
                                lambda t: (prev(t) % ni, prev(t) // ni))] + cast_out,
        out_shape=[jax.ShapeDtypeStruct((s, s5_ch), _BF16)] + cast_shapes,
        scratch_shapes=[pltpu.VMEM((HALVES, PERM_TILE, 2 * HALF_ST), _F32),
                        pltpu.VMEM((HALVES, PERM_TILE, 2 * HALF_ST), _BF16),
                        pltpu.VMEM((PERM_TILE, bc), _F32),
                        pltpu.VMEM((PERM_TILE, bc), _F32),
                        pltpu.VMEM((HALVES, SUBLANES, 2 * HALF_ST), _F32),
                        pltpu.VMEM((SUBLANES, SCAN_LANES), _F32)],
        compiler_params=_params("arbitrary"),
        name="branch_b_s5",
    )(hn, w_ub, wb, wc, lam, p64, d_skip, *[w for job in cast_jobs for w in job])
    return outs[0], outs[1:]


def _s5_params(lam_re, lam_im, log_dt, b_re, b_im, c_re, c_im):
    g = lam_re.shape[0]
    nh = g // GROUPS_PER_HALF
    dt = jnp.exp(log_dt)[:, None]
    a, b = lam_re * dt, lam_im * dt
    mag = jnp.exp(a)
    lbr, lbi = mag * jnp.cos(b), mag * jnp.sin(b)
    den = lam_re * lam_re + lam_im * lam_im
    qr = ((lbr - 1.0) * lam_re + lbi * lam_im) / den
    qi = (lbi * lam_re - (lbr - 1.0) * lam_im) / den
    bbr = qr[..., None] * b_re - qi[..., None] * b_im
    bbi = qr[..., None] * b_im + qi[..., None] * b_re
    mag64 = jnp.exp(SCAN_CHUNK * a)
    p64r, p64i = mag64 * jnp.cos(SCAN_CHUNK * b), mag64 * jnp.sin(SCAN_CHUNK * b)

    def block_diag(m, rows_per_group, cols_per_group):
        t = jnp.tile(m, (1, 1, GROUPS_PER_HALF))
        r = lax.broadcasted_iota(jnp.int32, t.shape, 1) // rows_per_group
        c = lax.broadcasted_iota(jnp.int32, t.shape, 2) // cols_per_group
        return jnp.where(r == c, t, 0.0)

    def in_mat(m):
        m = m.reshape(nh, GROUPS_PER_HALF, S5_STATE, S5_GROUP_CH).transpose(0, 1, 3, 2)
        return block_diag(m.reshape(nh, HALF_CH, S5_STATE), S5_GROUP_CH, S5_STATE)

    def out_mat(m):
        m = m.reshape(nh, GROUPS_PER_HALF, S5_GROUP_CH, S5_STATE).transpose(0, 1, 3, 2)
        return block_diag(m.reshape(nh, HALF_ST, S5_GROUP_CH), S5_STATE, S5_GROUP_CH)

    wb = jnp.concatenate([in_mat(bbr), in_mat(bbi)], axis=2).astype(_BF16)
    wc = jnp.concatenate([out_mat(c_re), out_mat(-c_im)], axis=1).astype(_BF16)
    lanes = lambda r, m: jnp.concatenate([r.reshape(nh, 1, HALF_ST), m.reshape(nh, 1, HALF_ST)], axis=2)
    return wb, wc, lanes(lbr, lbi), lanes(p64r, p64i)


def _glu_kernel(g_ref, gj_ref, hn_ref, wg_ref, bg_ref, wz_ref, o_ref):
    gate = _sigmoid(_dot(g_ref[...], wg_ref[...]) + bg_ref[...])
    gate = gate * _silu(_dot(hn_ref[...], wz_ref[...]))
    o_ref[...] = (gj_ref[...].astype(_F32) * gate).astype(_BF16)


def _glu(g, hn, w_glu, b_glu, w_in, zb_col0, tm, tn):
    s, c = g.shape
    d = hn.shape[1]
    off = zb_col0 // tn
    return pl.pallas_call(
        _glu_kernel,
        grid=(c // tn, s // tm),
        in_specs=[pl.BlockSpec((tm, c), lambda j, i: (i, 0)),
                  pl.BlockSpec((tm, tn), lambda j, i: (i, j)),
                  pl.BlockSpec((tm, d), lambda j, i: (i, 0)),
                  _resident((c, tn), lambda j, i: (0, j)),
                  _resident((1, tn), lambda j, i: (0, j)),
                  _resident((d, tn), lambda j, i: (0, off + j))],
        out_specs=pl.BlockSpec((tm, tn), lambda j, i: (i, j)),
        out_shape=jax.ShapeDtypeStruct((s, c), _BF16),
        compiler_params=_params("parallel", "parallel"),
        name="branch_b_glu",
    )(g, g, hn, w_glu, b_glu, w_in)


def _merge_kernel(hn_ref, ya_ref, yb_ref, wga_ref, wgb_ref, wua_ref, wub_ref, o_ref, m_ref):
    hn = hn_ref[...]
    m = _sigmoid(_dot(hn, wga_ref[...])) * _dot(ya_ref[...], wua_ref[...])
    m = m + _sigmoid(_dot(hn, wgb_ref[...])) * _dot(yb_ref[...], wub_ref[...])
    n_slabs = m_ref.shape[0]
    for k in range(n_slabs):
        m_ref[k] = m[:, k * LANES:(k + 1) * LANES]
    for j in range(SUBLANES):
        for k in range(n_slabs):
            rows = m_ref[k, pl.ds(j, SCAN_CHUNK, stride=SUBLANES), :]
            o_ref[j * SCAN_CHUNK:(j + 1) * SCAN_CHUNK, k * LANES:(k + 1) * LANES] = rows.astype(_BF16)


def _merge(hn, ya, yb, w_in, ga_col0, w_up_a, w_up_b, tn):
    s, d = hn.shape
    ca, cb = ya.shape[1], yb.shape[1]
    nj = d // tn
    off_a = ga_col0 // tn
    off_b = off_a + nj
    return pl.pallas_call(
        _merge_kernel,
        grid=(nj, s // PERM_TILE),
        in_specs=[pl.BlockSpec((PERM_TILE, d), lambda j, i: (i, 0)),
                  pl.BlockSpec((PERM_TILE, ca), lambda j, i: (i, 0)),
                  pl.BlockSpec((PERM_TILE, cb), lambda j, i: (i, 0)),
                  _resident((d, tn), lambda j, i: (0, off_a + j)),
                  _resident((d, tn), lambda j, i: (0, off_b + j)),
                  _resident((ca, tn), lambda j, i: (0, j)),
                  _resident((cb, tn), lambda j, i: (0, j))],
        out_specs=pl.BlockSpec((PERM_TILE, tn), lambda j, i: (i, j)),
        out_shape=jax.ShapeDtypeStruct((s, d), _BF16),
        scratch_shapes=[pltpu.VMEM((tn // LANES, PERM_TILE, LANES), _F32)],
        compiler_params=_params("parallel", "parallel"),
        name="merge_unpermute",
    )(hn, ya, yb, w_in, w_in, w_up_a, w_up_b)


def _out_kernel(m_ref, x_ref, w_ref, p_ref, wpp_ref, gp_ref, gf_ref, o_ref,
                hp_ref, ss1_ref, ss2_ref, *, n1, tn, final_norm):
    j = pl.program_id(1)
    nj = pl.num_programs(1)
    d = o_ref.shape[1]

    @pl.when(j == 0)
    def _():
        ss1_ref[...] = jnp.zeros_like(ss1_ref)
        ss2_ref[...] = jnp.zeros_like(ss2_ref)

    @pl.when(j < n1)
    def _():
        cols = pl.ds(pl.multiple_of(j * tn, tn), tn)
        h1 = x_ref[...] + _dot(m_ref[...], w_ref[...])
        o_ref[:, cols] = h1
        hp_ref[:, cols] = (h1 * gp_ref[:, cols]).astype(_BF16)
        ss1_ref[...] += jnp.sum(h1 * h1, axis=-1, keepdims=True)

    @pl.when(j >= n1)
    def _():
        cols = pl.ds(pl.multiple_of((j - n1) * tn, tn), tn)
        rinv = lax.rsqrt(ss1_ref[...] * (1.0 / d) + EPS)
        gate = _sigmoid(rinv * _dot(hp_ref[...], w_ref[...]))
        emb = _dot(p_ref[...].astype(_BF16), wpp_ref[...])
        h2 = o_ref[:, cols] + gate * emb
        o_ref[:, cols] = h2
        ss2_ref[...] += jnp.sum(h2 * h2, axis=-1, keepdims=True)

    if final_norm:
        @pl.when(j == nj - 1)
        def _():
            rinv = lax.rsqrt(ss2_ref[...] * (1.0 / d) + EPS)
            o_ref[...] = o_ref[...] * rinv * gf_ref[...]


def _out(merged, x2, w_cat, p2, w_pp, g_ple, g_final, tm, tn, final_norm):
    s, d = x2.shape
    pd = p2.shape[1]
    n1 = d // tn
    first = lambda j: jnp.minimum(j, n1 - 1)
    second = lambda j: jnp.maximum(j - n1, 0)
    return pl.pallas_call(
        functools.partial(_out_kernel, n1=n1, tn=tn, final_norm=final_norm),
        grid=(s // tm, 2 * n1),
        in_specs=[pl.BlockSpec((tm, d), lambda i, j: (i, 0)),
                  pl.BlockSpec((tm, tn), lambda i, j: (i, first(j))),
                  pl.BlockSpec((d, tn), lambda i, j: (0, j)),
                  pl.BlockSpec((tm, pd), lambda i, j: (i, 0)),
                  pl.BlockSpec((pd, tn), lambda i, j: (0, second(j))),
                  pl.BlockSpec((1, d), lambda i, j: (0, 0)),
                  pl.BlockSpec((1, d), lambda i, j: (0, 0))],
        out_specs=pl.BlockSpec((tm, d), lambda i, j: (i, 0)),
        out_shape=jax.ShapeDtypeStruct((s, d), _F32),
        scratch_shapes=[pltpu.VMEM((tm, d), _BF16),
                        pltpu.VMEM((tm, 1), _F32),
                        pltpu.VMEM((tm, 1), _F32)],
        compiler_params=_params("parallel", "arbitrary"),
        name="out_proj_ple_norm",
    )(merged, x2, w_cat, p2, w_pp, g_ple, g_final)


def kernel(x, p, norm_in_g, w_in, conv_w, conv_b, lam_re, lam_im, log_dt, b_re, b_im, c_re, c_im,
           d_skip, w_glu, b_glu, w_up_conv, w_up_s5, w_out, ple_norm_g, w_ple_gate, w_ple_proj,
           final_norm_g):
    bsz, seq, d = x.shape
    depth = w_in.shape[0]
    conv_ch = conv_w.shape[-1]
    s5_ch = d_skip.shape[-1]
    assert bsz == 1 and seq % PERM_TILE == 0 and s5_ch % (HALVES * HALF_CH) == 0
    col_ub = 4 * conv_ch
    col_zb = col_ub + s5_ch
    col_ga = col_zb + s5_ch
    tm_big = 2 * PERM_TILE if seq % (2 * PERM_TILE) == 0 else PERM_TILE
    tn = min(512, conv_ch, s5_ch)
    tn_wide = min(1024, d)

    pm = jnp.asarray(_perm_matrix(), _BF16)

    h = x.reshape(seq, d)
    for i in range(depth):
        last = i == depth - 1
        wb, wc, lam_l, p64 = _s5_params(lam_re[i], lam_im[i], log_dt[i], b_re[i], b_im[i],
                                        c_re[i], c_im[i])
        hn = _prep(h, norm_in_g[i][None, :], pm)

        w_ub = w_in[i][:, col_ub:col_zb].astype(_BF16)
        g, (w_in_b,) = _s5(hn, w_ub, wb, wc, lam_l, p64, d_skip[i][None, :], [(w_in[i],)])
        ya, (w_glu_b, w_ua_b, w_us_b, w_cat) = _branch_a(
            hn, w_in_b, conv_w[i], conv_b[i][None, :], conv_ch, tm_big, tn,
            [(w_glu[i],), (w_up_conv[i],), (w_up_s5[i],), (w_out[i], w_ple_gate[i])])

        yb = _glu(g, hn, w_glu_b, b_glu[i][None, :], w_in_b, col_zb, tm_big, tn)
        merged = _merge(hn, ya, yb, w_in_b, col_ga, w_ua_b, w_us_b, tn_wide)
        h = _out(merged, h, w_cat, p[i].reshape(seq, -1), w_ple_proj[i].astype(_BF16),
                 ple_norm_g[i][None, :], final_norm_g[None, :], PERM_TILE, tn_wide, last)
    return h.reshape(bsz, seq, d)
```

```python
import functools

import numpy as np
import jax
import jax.numpy as jnp
from jax import lax
from jax.experimental import pallas as pl
from jax.experimental.pallas import tpu as pltpu

EPS = 1e-6
LANES = 128
SUBLANES = 8
BF16_ROWS = 16
SCAN_CHUNK = 64
PERM_TILE = SUBLANES * SCAN_CHUNK
S5_GROUP_CH = 16
S5_STATE = 64
GROUPS_PER_HALF = 16
HALF_CH = GROUPS_PER_HALF * S5_GROUP_CH
HALF_ST = GROUPS_PER_HALF * S5_STATE
HALVES = 2
SCAN_LANES = 512
VMEM_LIMIT = 60 * 1024 * 1024

_BF16 = jnp.bfloat16
_F32 = jnp.float32


def _dot(a, b):
    return jnp.dot(a, b, preferred_element_type=_F32)


def _sigmoid(v):
    return jax.nn.sigmoid(v)


def _silu(v):
    return v * jax.nn.sigmoid(v)


def _perm_matrix():
    pm = np.zeros((PERM_TILE, PERM_TILE), np.float32)
    for j in range(SUBLANES):
        for l in range(SCAN_CHUNK):
            pm[l * SUBLANES + j, j * SCAN_CHUNK + l] = 1.0
    return pm


def _params(*sem):
    return pltpu.CompilerParams(dimension_semantics=sem, vmem_limit_bytes=VMEM_LIMIT)


def _resident(shape, index_map):
    return pl.BlockSpec(shape, index_map, pipeline_mode=pl.Buffered(1))


def _cast_specs(jobs, steps, slab_index):
    in_specs, out_specs, out_shapes = [], [], []
    slab = lambda *idx: (slab_index(*idx), 0)
    for job in jobs:
        rows = job[0].shape[0]
        assert rows % (steps * BF16_ROWS) == 0 and all(w.shape[0] == rows for w in job)
        cols = sum(w.shape[1] for w in job)
        in_specs += [pl.BlockSpec((rows // steps, w.shape[1]), slab) for w in job]
        out_specs.append(pl.BlockSpec((rows // steps, cols), slab))
        out_shapes.append(jax.ShapeDtypeStruct((rows, cols), _BF16))
    return in_specs, out_specs, out_shapes


def _cast_all(src_refs, dst_refs):
    src = iter(src_refs)
    for dst in dst_refs:
        col = 0
        while col < dst.shape[1]:
            w = next(src)
            dst[:, col:col + w.shape[1]] = w[...].astype(_BF16)
            col += w.shape[1]


def _prep_kernel(x_ref, g_ref, pm_ref, o_ref):
    x = x_ref[...]
    ms = jnp.mean(x * x, axis=-1, keepdims=True)
    hn = (x * lax.rsqrt(ms + EPS) * g_ref[...]).astype(_BF16)
    o_ref[...] = _dot(pm_ref[...], hn).astype(_BF16)


def _prep(x2, gain, pm):
    s, d = x2.shape
    return pl.pallas_call(
        _prep_kernel,
        grid=(s // PERM_TILE,),
        in_specs=[pl.BlockSpec((PERM_TILE, d), lambda i: (i, 0)),
                  pl.BlockSpec((1, d), lambda i: (0, 0)),
                  pl.BlockSpec((PERM_TILE, PERM_TILE), lambda i: (0, 0))],
        out_specs=pl.BlockSpec((PERM_TILE, d), lambda i: (i, 0)),
        out_shape=jax.ShapeDtypeStruct((s, d), _BF16),
        compiler_params=_params("parallel"),
        name="prep_norm_permute",
    )(x2, gain, pm)


def _branch_a_kernel(*refs, n_src, n_dst):
    hn_ref, wx_ref, wc_ref, wb_ref, wz_ref, cw_ref, cb_ref = refs[:7]
    cast_src = refs[7:7 + n_src]
    o_ref = refs[7 + n_src]
    cast_dst = refs[8 + n_src:8 + n_src + n_dst]
    v_ref, tail_ref = refs[8 + n_src + n_dst:]
    i = pl.program_id(1)

    @pl.when(i == 0)
    def _():
        tail_ref[...] = jnp.zeros_like(tail_ref)

    _cast_all(cast_src, cast_dst)
    hn = hn_ref[...]
    v_ref[...] = _dot(hn, wc_ref[...]) * _dot(hn, wx_ref[...])
    gate = _dot(hn, wb_ref[...]) * _silu(_dot(hn, wz_ref[...]))
    tm, tn = v_ref.shape
    sub = lax.broadcasted_iota(jnp.int32, (SUBLANES, tn), 0)
    cw = cw_ref[...]

    def from_previous_chunk(cur, prev_last):
        return jnp.where(sub == 0, prev_last, pltpu.roll(cur, 1, 0))

    for t in range(tm // PERM_TILE):
        r0 = t * PERM_TILE
        v = v_ref[r0:r0 + PERM_TILE, :]
        tail = v[PERM_TILE - 2 * SUBLANES:, :]
        prev = tail_ref[...]
        r62 = from_previous_chunk(tail[:SUBLANES], prev[SUBLANES - 1:SUBLANES])
        r63 = from_previous_chunk(tail[SUBLANES:], prev[2 * SUBLANES - 1:])
        tail_ref[...] = tail
        vs1 = jnp.concatenate([r63, v[:PERM_TILE - SUBLANES]], axis=0)
        vs2 = jnp.concatenate([r62, r63, v[:PERM_TILE - 2 * SUBLANES]], axis=0)
        conv = cw[0:1] * vs2 + cw[1:2] * vs1 + cw[2:3] * v + cb_ref[...]
        o_ref[r0:r0 + PERM_TILE, :] = (gate[r0:r0 + PERM_TILE, :] * conv).astype(_BF16)


def _branch_a(hn, w_in, conv_w, conv_b, conv_ch, tm, tn, cast_jobs):
    s, d = hn.shape
    nj, ni = conv_ch // tn, s // tm
    wspec = lambda k: _resident((d, tn), lambda j, i, k=k: (0, k * nj + j))
    cast_in, cast_out, cast_shapes = _cast_specs(cast_jobs, nj * ni, lambda j, i: j * ni + i)
    outs = pl.pallas_call(
        functools.partial(_branch_a_kernel, n_src=len(cast_in), n_dst=len(cast_out)),
        grid=(nj, ni),
        in_specs=[pl.BlockSpec((tm, d), lambda j, i: (i, 0)),
                  wspec(0), wspec(1), wspec(2), wspec(3),
                  _resident((conv_w.shape[0], tn), lambda j, i: (0, j)),
                  _resident((1, tn), lambda j, i: (0, j))] + cast_in,
        out_specs=[pl.BlockSpec((tm, tn), lambda j, i: (i, j))] + cast_out,
        out_shape=[jax.ShapeDtypeStruct((s, conv_ch), _BF16)] + cast_shapes,
        scratch_shapes=[pltpu.VMEM((tm, tn), _F32),
                        pltpu.VMEM((2 * SUBLANES, tn), _F32)],
        compiler_params=_params("arbitrary", "arbitrary"),
        name="branch_a_conv",
    )(hn, w_in, w_in, w_in, w_in, conv_w, conv_b, *[w for job in cast_jobs for w in job])
    return outs[0], outs[1:]


def _s5_kernel(*refs, n_src, n_dst, tiles_per_block):
    hn_ref, wu_ref, wb_ref, wc_ref, lam_ref, p64_ref, d_ref = refs[:7]
    cast_src = refs[7:7 + n_src]
    o_ref = refs[7 + n_src]
    cast_dst = refs[8 + n_src:8 + n_src + n_dst]
    bu_ref, hb_ref, un_ref, uo_ref, st_ref, tok_ref = refs[8 + n_src + n_dst:]
    s = pl.program_id(0)
    w = HALF_ST
    shape = (SUBLANES, SCAN_LANES)
    tok_ref[...] = jnp.zeros_like(tok_ref)

    @pl.when(s == 0)
    def _():
        bu_ref[...] = jnp.zeros_like(bu_ref)
        uo_ref[...] = jnp.zeros_like(uo_ref)

    @pl.when(jnp.maximum(s - 1, 0) % tiles_per_block == 0)
    def _():
        st_ref[...] = jnp.zeros_like(st_ref)

    _cast_all(cast_src, cast_dst)
    un_ref[...] = _dot(hn_ref[...], wu_ref[...])

    for h in range(HALVES):
        for g in range(w // SCAN_LANES):
            re = slice(g * SCAN_LANES, (g + 1) * SCAN_LANES)
            im = slice(w + g * SCAN_LANES, w + (g + 1) * SCAN_LANES)
            lr = jnp.broadcast_to(lam_ref[h, :, re], shape)
            li = jnp.broadcast_to(lam_ref[h, :, im], shape)

            def advance(hr, hi, l):
                rows = slice(l * SUBLANES, (l + 1) * SUBLANES)
                return (lr * hr - li * hi + bu_ref[h, rows, re],
                        lr * hi + li * hr + bu_ref[h, rows, im])

            er = ei = jnp.where(s < 0, tok_ref[...], 0.0)
            for l in range(SCAN_CHUNK):
                er, ei = advance(er, ei, l)

            pr = p64_ref[h, :, re]
            pi = p64_ref[h, :, im]
            cr = st_ref[h, 0:1, re]
            ci = st_ref[h, 0:1, im]
            rows_r, rows_i = [], []
            for j in range(SUBLANES):
                rows_r.append(cr)
                rows_i.append(ci)
                cr, ci = (er[j:j + 1] + (pr * cr - pi * ci),
                          ei[j:j + 1] + (pr * ci + pi * cr))
            st_ref[h, 0:1, re] = cr
            st_ref[h, 0:1, im] = ci
            hr = jnp.concatenate(rows_r, axis=0)
            hi = jnp.concatenate(rows_i, axis=0)

            for l in range(0, SCAN_CHUNK, 2):
                ar, ai = advance(hr, hi, l)
                hr, hi = advance(ar, ai, l + 1)
                rows = slice(l * SUBLANES, (l + 2) * SUBLANES)
                hb_ref[h, rows, re] = jnp.concatenate([ar, hr], axis=0).astype(_BF16)
                hb_ref[h, rows, im] = jnp.concatenate([ai, hi], axis=0).astype(_BF16)
            tok_ref[...] = hr

    y = jnp.concatenate([_dot(hb_ref[h], wc_ref[h]) for h in range(HALVES)], axis=1)
    o_ref[...] = jax.nn.gelu(y + d_ref[...] * uo_ref[...]).astype(_BF16)
    u = un_ref[...]
    uo_ref[...] = u
    ub = u.astype(_BF16)
    for h in range(HALVES):
        bu_ref[h] = _dot(ub[:, h * HALF_CH:(h + 1) * HALF_CH], wb_ref[h])


def _s5(hn, w_ub, wb, wc, lam, p64, d_skip, cast_jobs):
    s, d = hn.shape
    s5_ch = w_ub.shape[1]
    bc = HALVES * HALF_CH
    nb, ni = s5_ch // bc, s // PERM_TILE
    tiles = nb * ni
    cur = lambda t: jnp.minimum(t, tiles - 1)
    prev = lambda t: jnp.maximum(t - 1, 0)
    cast_in, cast_out, cast_shapes = _cast_specs(cast_jobs, tiles, cur)
    outs = pl.pallas_call(
        functools.partial(_s5_kernel, n_src=len(cast_in), n_dst=len(cast_out),
                          tiles_per_block=ni),
        grid=(tiles + 1,),
        in_specs=[pl.BlockSpec((PERM_TILE, d), lambda t: (cur(t) % ni, 0)),
                  _resident((d, bc), lambda t: (0, cur(t) // ni)),
                  _resident((HALVES, HALF_CH, 2 * HALF_ST), lambda t: (cur(t) // ni, 0, 0)),
                  _resident((HALVES, 2 * HALF_ST, HALF_CH), lambda t: (prev(t) // ni, 0, 0)),
                  _resident((HALVES, 1, 2 * HALF_ST), lambda t: (prev(t) // ni, 0, 0)),
                  _resident((HALVES, 1, 2 * HALF_ST), lambda t: (prev(t) // ni, 0, 0)),
                  _resident((1, bc), lambda t: (0, prev(t) // ni))] + cast_in,
        out_specs=[pl.BlockSpec((PERM_TILE, bc),
                                lambda t: (prev(t) % ni, prev(t) // ni))] + cast_out,
        out_shape=[jax.ShapeDtypeStruct((s, s5_ch), _BF16)] + cast_shapes,
        scratch_shapes=[pltpu.VMEM((HALVES, PERM_TILE, 2 * HALF_ST), _F32),
                        pltpu.VMEM((HALVES, PERM_TILE, 2 * HALF_ST), _BF16),
                        pltpu.VMEM((PERM_TILE, bc), _F32),
                        pltpu.VMEM((PERM_TILE, bc), _F32),
                        pltpu.VMEM((HALVES, SUBLANES, 2 * HALF_ST), _F32),
                        pltpu.VMEM((SUBLANES, SCAN_LANES), _F32)],
        compiler_params=_params("arbitrary"),
        name="branch_b_s5",
    )(hn, w_ub, wb, wc, lam, p64, d_skip, *[w for job in cast_jobs for w in job])
    return outs[0], outs[1:]


def _s5_params(lam_re, lam_im, log_dt, b_re, b_im, c_re, c_im):
    g = lam_re.shape[0]
    nh = g // GROUPS_PER_HALF
    dt = jnp.exp(log_dt)[:, None]
    a, b = lam_re * dt, lam_im * dt
    mag = jnp.exp(a)
    lbr, lbi = mag * jnp.cos(b), mag * jnp.sin(b)
    den = lam_re * lam_re + lam_im * lam_im
    qr = ((lbr - 1.0) * lam_re + lbi * lam_im) / den
    qi = (lbi * lam_re - (lbr - 1.0) * lam_im) / den
    bbr = qr[..., None] * b_re - qi[..., None] * b_im
    bbi = qr[..., None] * b_im + qi[..., None] * b_re
    mag64 = jnp.exp(SCAN_CHUNK * a)
    p64r, p64i = mag64 * jnp.cos(SCAN_CHUNK * b), mag64 * jnp.sin(SCAN_CHUNK * b)

    gp = GROUPS_PER_HALF
    same_group = jnp.eye(gp, dtype=bool)
    bb = jnp.stack([bbr, bbi], axis=0).reshape(2, nh, gp, S5_STATE, S5_GROUP_CH)
    bb = bb.transpose(1, 2, 4, 0, 3)[:, :, :, :, None, :]
    wb = jnp.where(same_group[None, :, None, None, :, None], bb, 0.0)
    wb = wb.astype(_BF16).reshape(nh, HALF_CH, 2 * HALF_ST)
    cc = jnp.stack([c_re, -c_im], axis=0).reshape(2, nh, gp, S5_GROUP_CH, S5_STATE)
    cc = cc.transpose(1, 0, 2, 4, 3)[:, :, :, :, None, :]
    wc = jnp.where(same_group[None, None, :, None, :, None], cc, 0.0)
    wc = wc.astype(_BF16).reshape(nh, 2 * HALF_ST, HALF_CH)
    lanes = lambda r, m: jnp.concatenate([r.reshape(nh, 1, HALF_ST), m.reshape(nh, 1, HALF_ST)], axis=2)
    return wb, wc, lanes(lbr, lbi), lanes(p64r, p64i)


def _glu_kernel(g_ref, gj_ref, hn_ref, wg_ref, bg_ref, wz_ref, o_ref):
    gate = _sigmoid(_dot(g_ref[...], wg_ref[...]) + bg_ref[...])
    gate = gate * _silu(_dot(hn_ref[...], wz_ref[...]))
    o_ref[...] = (gj_ref[...].astype(_F32) * gate).astype(_BF16)


def _glu(g, hn, w_glu, b_glu, w_in, zb_col0, tm, tn):
    s, c = g.shape
    d = hn.shape[1]
    off = zb_col0 // tn
    return pl.pallas_call(
        _glu_kernel,
        grid=(c // tn, s // tm),
        in_specs=[pl.BlockSpec((tm, c), lambda j, i: (i, 0)),
                  pl.BlockSpec((tm, tn), lambda j, i: (i, j)),
                  pl.BlockSpec((tm, d), lambda j, i: (i, 0)),
                  _resident((c, tn), lambda j, i: (0, j)),
                  _resident((1, tn), lambda j, i: (0, j)),
                  _resident((d, tn), lambda j, i: (0, off + j))],
        out_specs=pl.BlockSpec((tm, tn), lambda j, i: (i, j)),
        out_shape=jax.ShapeDtypeStruct((s, c), _BF16),
        compiler_params=_params("parallel", "parallel"),
        name="branch_b_glu",
    )(g, g, hn, w_glu, b_glu, w_in)


def _merge_kernel(hn_ref, ya_ref, yb_ref, wga_ref, wgb_ref, wua_ref, wub_ref, o_ref, m_ref):
    hn = hn_ref[...]
    m = _sigmoid(_dot(hn, wga_ref[...])) * _dot(ya_ref[...], wua_ref[...])
    m = m + _sigmoid(_dot(hn, wgb_ref[...])) * _dot(yb_ref[...], wub_ref[...])
    n_slabs = m_ref.shape[0]
    for k in range(n_slabs):
        m_ref[k] = m[:, k * LANES:(k + 1) * LANES]
    for j in range(SUBLANES):
        for k in range(n_slabs):
            rows = m_ref[k, pl.ds(j, SCAN_CHUNK, stride=SUBLANES), :]
            o_ref[j * SCAN_CHUNK:(j + 1) * SCAN_CHUNK, k * LANES:(k + 1) * LANES] = rows.astype(_BF16)


def _merge(hn, ya, yb, w_in, ga_col0, w_up_a, w_up_b, tn):
    s, d = hn.shape
    ca, cb = ya.shape[1], yb.shape[1]
    nj = d // tn
    off_a = ga_col0 // tn
    off_b = off_a + nj
    return pl.pallas_call(
        _merge_kernel,
        grid=(nj, s // PERM_TILE),
        in_specs=[pl.BlockSpec((PERM_TILE, d), lambda j, i: (i, 0)),
                  pl.BlockSpec((PERM_TILE, ca), lambda j, i: (i, 0)),
                  pl.BlockSpec((PERM_TILE, cb), lambda j, i: (i, 0)),
                  _resident((d, tn), lambda j, i: (0, off_a + j)),
                  _resident((d, tn), lambda j, i: (0, off_b + j)),
                  _resident((ca, tn), lambda j, i: (0, j)),
                  _resident((cb, tn), lambda j, i: (0, j))],
        out_specs=pl.BlockSpec((PERM_TILE, tn), lambda j, i: (i, j)),
        out_shape=jax.ShapeDtypeStruct((s, d), _BF16),
        scratch_shapes=[pltpu.VMEM((tn // LANES, PERM_TILE, LANES), _F32)],
        compiler_params=_params("parallel", "parallel"),
        name="merge_unpermute",
    )(hn, ya, yb, w_in, w_in, w_up_a, w_up_b)


def _out_kernel(m_ref, x_ref, w_ref, p_ref, wpp_ref, gp_ref, gf_ref, o_ref,
                hp_ref, ss1_ref, ss2_ref, *, n1, tn, final_norm):
    j = pl.program_id(1)
    nj = pl.num_programs(1)
    d = o_ref.shape[1]

    @pl.when(j == 0)
    def _():
        ss1_ref[...] = jnp.zeros_like(ss1_ref)
        ss2_ref[...] = jnp.zeros_like(ss2_ref)

    @pl.when(j < n1)
    def _():
        cols = pl.ds(pl.multiple_of(j * tn, tn), tn)
        h1 = x_ref[...] + _dot(m_ref[...], w_ref[...])
        o_ref[:, cols] = h1
        hp_ref[:, cols] = (h1 * gp_ref[:, cols]).astype(_BF16)
        ss1_ref[...] += jnp.sum(h1 * h1, axis=-1, keepdims=True)

    @pl.when(j >= n1)
    def _():
        cols = pl.ds(pl.multiple_of((j - n1) * tn, tn), tn)
        rinv = lax.rsqrt(ss1_ref[...] * (1.0 / d) + EPS)
        gate = _sigmoid(rinv * _dot(hp_ref[...], w_ref[...]))
        emb = _dot(p_ref[...].astype(_BF16), wpp_ref[...])
        h2 = o_ref[:, cols] + gate * emb
        o_ref[:, cols] = h2
        ss2_ref[...] += jnp.sum(h2 * h2, axis=-1, keepdims=True)

    if final_norm:
        @pl.when(j == nj - 1)
        def _():
            rinv = lax.rsqrt(ss2_ref[...] * (1.0 / d) + EPS)
            o_ref[...] = o_ref[...] * rinv * gf_ref[...]


def _out(merged, x2, w_cat, p2, w_pp, g_ple, g_final, tm, tn, final_norm):
    s, d = x2.shape
    pd = p2.shape[1]
    n1 = d // tn
    first = lambda j: jnp.minimum(j, n1 - 1)
    second = lambda j: jnp.maximum(j - n1, 0)
    return pl.pallas_call(
        functools.partial(_out_kernel, n1=n1, tn=tn, final_norm=final_norm),
        grid=(s // tm, 2 * n1),
        in_specs=[pl.BlockSpec((tm, d), lambda i, j: (i, 0)),
                  pl.BlockSpec((tm, tn), lambda i, j: (i, first(j))),
                  pl.BlockSpec((d, tn), lambda i, j: (0, j)),
                  pl.BlockSpec((tm, pd), lambda i, j: (i, 0)),
                  pl.BlockSpec((pd, tn), lambda i, j: (0, second(j))),
                  pl.BlockSpec((1, d), lambda i, j: (0, 0)),
                  pl.BlockSpec((1, d), lambda i, j: (0, 0))],
        out_specs=pl.BlockSpec((tm, d), lambda i, j: (i, 0)),
        out_shape=jax.ShapeDtypeStruct((s, d), _F32),
        scratch_shapes=[pltpu.VMEM((tm, d), _BF16),
                        pltpu.VMEM((tm, 1), _F32),
                        pltpu.VMEM((tm, 1), _F32)],
        compiler_params=_params("parallel", "arbitrary"),
        name="out_proj_ple_norm",
    )(merged, x2, w_cat, p2, w_pp, g_ple, g_final)


def kernel(x, p, norm_in_g, w_in, conv_w, conv_b, lam_re, lam_im, log_dt, b_re, b_im, c_re, c_im,
           d_skip, w_glu, b_glu, w_up_conv, w_up_s5, w_out, ple_norm_g, w_ple_gate, w_ple_proj,
           final_norm_g):
    bsz, seq, d = x.shape
    depth = w_in.shape[0]
    conv_ch = conv_w.shape[-1]
    s5_ch = d_skip.shape[-1]
    assert bsz == 1 and seq % PERM_TILE == 0 and s5_ch % (HALVES * HALF_CH) == 0
    col_ub = 4 * conv_ch
    col_zb = col_ub + s5_ch
    col_ga = col_zb + s5_ch
    tm_big = 2 * PERM_TILE if seq % (2 * PERM_TILE) == 0 else PERM_TILE
    tn = min(512, conv_ch, s5_ch)
    tn_wide = min(1024, d)

    pm = jnp.asarray(_perm_matrix(), _BF16)

    h = x.reshape(seq, d)
    for i in range(depth):
        last = i == depth - 1
        wb, wc, lam_l, p64 = _s5_params(lam_re[i], lam_im[i], log_dt[i], b_re[i], b_im[i],
                                        c_re[i], c_im[i])
        hn = _prep(h, norm_in_g[i][None, :], pm)

        w_ub = w_in[i][:, col_ub:col_zb].astype(_BF16)
        g, (w_in_b,) = _s5(hn, w_ub, wb, wc, lam_l, p64, d_skip[i][None, :], [(w_in[i],)])
        ya, (w_glu_b, w_ua_b, w_us_b, w_cat) = _branch_a(
            hn, w_in_b, conv_w[i], conv_b[i][None, :], conv_ch, tm_big, tn,
            [(w_glu[i],), (w_up_conv[i],), (w_up_s5[i],), (w_out[i], w_ple_gate[i])])

        yb = _glu(g, hn, w_glu_b, b_glu[i][None, :], w_in_b, col_zb, tm_big, tn)
        merged = _merge(hn, ya, yb, w_in_b, col_ga, w_ua_b, w_us_b, tn_wide)
        h = _out(merged, h, w_cat, p[i].reshape(seq, -1), w_ple_proj[i].astype(_BF16),
                 ple_norm_g[i][None, :], final_norm_g[None, :], PERM_TILE, tn_wide, last)
    return h.reshape(bsz, seq, d)
```

```python
import functools

import numpy as np
import jax
import jax.numpy as jnp
from jax import lax
from jax.experimental import pallas as pl
from jax.experimental.pallas import tpu as pltpu

EPS = 1e-6
LANES = 128
SUBLANES = 8
BF16_ROWS = 16
SCAN_CHUNK = 64
PERM_TILE = SUBLANES * SCAN_CHUNK
S5_GROUP_CH = 16
S5_STATE = 64
GROUPS_PER_HALF = 16
HALF_CH = GROUPS_PER_HALF * S5_GROUP_CH
HALF_ST = GROUPS_PER_HALF * S5_STATE
HALVES = 2
SCAN_LANES = 512
VMEM_LIMIT = 60 * 1024 * 1024

_BF16 = jnp.bfloat16
_F32 = jnp.float32


def _dot(a, b):
    return jnp.dot(a, b, preferred_element_type=_F32)


def _sigmoid(v):
    return jax.nn.sigmoid(v)


def _silu(v):
    return v * jax.nn.sigmoid(v)


def _perm_matrix():
    pm = np.zeros((PERM_TILE, PERM_TILE), np.float32)
    for j in range(SUBLANES):
        for l in range(SCAN_CHUNK):
            pm[l * SUBLANES + j, j * SCAN_CHUNK + l] = 1.0
    return pm


def _params(*sem):
    return pltpu.CompilerParams(dimension_semantics=sem, vmem_limit_bytes=VMEM_LIMIT)


def _resident(shape, index_map):
    return pl.BlockSpec(shape, index_map, pipeline_mode=pl.Buffered(1))


def _cast_specs(jobs, steps, slab_index):
    in_specs, out_specs, out_shapes = [], [], []
    slab = lambda *idx: (slab_index(*idx), 0)
    for job in jobs:
        rows = job[0].shape[0]
        assert rows % (steps * BF16_ROWS) == 0 and all(w.shape[0] == rows for w in job)
        cols = sum(w.shape[1] for w in job)
        in_specs += [pl.BlockSpec((rows // steps, w.shape[1]), slab) for w in job]
        out_specs.append(pl.BlockSpec((rows // steps, cols), slab))
        out_shapes.append(jax.ShapeDtypeStruct((rows, cols), _BF16))
    return in_specs, out_specs, out_shapes


def _cast_all(src_refs, dst_refs):
    src = iter(src_refs)
    for dst in dst_refs:
        col = 0
        while col < dst.shape[1]:
            w = next(src)
            dst[:, col:col + w.shape[1]] = w[...].astype(_BF16)
            col += w.shape[1]


def _prep_kernel(x_ref, g_ref, pm_ref, o_ref):
    x = x_ref[...]
    ms = jnp.mean(x * x, axis=-1, keepdims=True)
    hn = (x * lax.rsqrt(ms + EPS) * g_ref[...]).astype(_BF16)
    o_ref[...] = _dot(pm_ref[...], hn).astype(_BF16)


def _prep(x2, gain, pm):
    s, d = x2.shape
    return pl.pallas_call(
        _prep_kernel,
        grid=(s // PERM_TILE,),
        in_specs=[pl.BlockSpec((PERM_TILE, d), lambda i: (i, 0)),
                  pl.BlockSpec((1, d), lambda i: (0, 0)),
                  pl.BlockSpec((PERM_TILE, PERM_TILE), lambda i: (0, 0))],
        out_specs=pl.BlockSpec((PERM_TILE, d), lambda i: (i, 0)),
        out_shape=jax.ShapeDtypeStruct((s, d), _BF16),
        compiler_params=_params("parallel"),
        name="prep_norm_permute",
    )(x2, gain, pm)


def _branch_a_kernel(*refs, n_src, n_dst):
    hn_ref, wx_ref, wc_ref, wb_ref, wz_ref, cw_ref, cb_ref = refs[:7]
    cast_src = refs[7:7 + n_src]
    o_ref = refs[7 + n_src]
    cast_dst = refs[8 + n_src:8 + n_src + n_dst]
    v_ref, tail_ref = refs[8 + n_src + n_dst:]
    i = pl.program_id(1)

    @pl.when(i == 0)
    def _():
        tail_ref[...] = jnp.zeros_like(tail_ref)

    _cast_all(cast_src, cast_dst)
    hn = hn_ref[...]
    v_ref[...] = _dot(hn, wc_ref[...]) * _dot(hn, wx_ref[...])
    za = _dot(hn, wz_ref[...])
    tm, tn = v_ref.shape
    sub = lax.broadcasted_iota(jnp.int32, (SUBLANES, tn), 0)
    cw = cw_ref[...]

    def from_previous_chunk(cur, prev_last):
        return jnp.where(sub == 0, prev_last, pltpu.roll(cur, 1, 0))

    for t in range(tm // PERM_TILE):
        r0 = t * PERM_TILE
        v = v_ref[r0:r0 + PERM_TILE, :]
        tail = v[PERM_TILE - 2 * SUBLANES:, :]
        prev = tail_ref[...]
        r62 = from_previous_chunk(tail[:SUBLANES], prev[SUBLANES - 1:SUBLANES])
        r63 = from_previous_chunk(tail[SUBLANES:], prev[2 * SUBLANES - 1:])
        tail_ref[...] = tail
        vs1 = jnp.concatenate([r63, v[:PERM_TILE - SUBLANES]], axis=0)
        vs2 = jnp.concatenate([r62, r63, v[:PERM_TILE - 2 * SUBLANES]], axis=0)
        conv = cw[0:1] * vs2 + cw[1:2] * vs1 + cw[2:3] * v + cb_ref[...]
        v_ref[r0:r0 + PERM_TILE, :] = conv * _silu(za[r0:r0 + PERM_TILE, :])
    o_ref[...] = (_dot(hn, wb_ref[...]) * v_ref[...]).astype(_BF16)


def _branch_a(hn, w_in, conv_w, conv_b, conv_ch, tm, tn, cast_jobs):
    s, d = hn.shape
    nj, ni = conv_ch // tn, s // tm
    wspec = lambda k: _resident((d, tn), lambda j, i, k=k: (0, k * nj + j))
    cast_in, cast_out, cast_shapes = _cast_specs(cast_jobs, nj * ni, lambda j, i: j * ni + i)
    outs = pl.pallas_call(
        functools.partial(_branch_a_kernel, n_src=len(cast_in), n_dst=len(cast_out)),
        grid=(nj, ni),
        in_specs=[pl.BlockSpec((tm, d), lambda j, i: (i, 0)),
                  wspec(0), wspec(1), wspec(2), wspec(3),
                  _resident((conv_w.shape[0], tn), lambda j, i: (0, j)),
                  _resident((1, tn), lambda j, i: (0, j))] + cast_in,
        out_specs=[pl.BlockSpec((tm, tn), lambda j, i: (i, j))] + cast_out,
        out_shape=[jax.ShapeDtypeStruct((s, conv_ch), _BF16)] + cast_shapes,
        scratch_shapes=[pltpu.VMEM((tm, tn), _F32),
                        pltpu.VMEM((2 * SUBLANES, tn), _F32)],
        compiler_params=_params("arbitrary", "arbitrary"),
        name="branch_a_conv",
    )(hn, w_in, w_in, w_in, w_in, conv_w, conv_b, *[w for job in cast_jobs for w in job])
    return outs[0], outs[1:]


def _s5_kernel(*refs, n_src, n_dst, tiles_per_block, n_tiles):
    hn_ref, wu_ref, cb_ref, cc_ref, lam_ref, p64_ref, d_ref, e_ref, et_ref = refs[:9]
    cast_src = refs[9:9 + n_src]
    o_ref = refs[9 + n_src]
    cast_dst = refs[10 + n_src:10 + n_src + n_dst]
    bu_ref, hb_ref, un_ref, uo_ref, st_ref, tok_ref, wb_ref, wc_ref = refs[10 + n_src + n_dst:]
    s = pl.program_id(0)
    w = HALF_ST
    shape = (SUBLANES, SCAN_LANES)
    tok_ref[...] = jnp.zeros_like(tok_ref)

    def group_of(shape, axis, per_group):
        idx = lax.broadcasted_iota(jnp.int32, shape, axis)
        return (idx % (GROUPS_PER_HALF * per_group)) // per_group

    @pl.when(jnp.minimum(s, n_tiles - 1) % tiles_per_block == 0)
    def _():
        keep = (group_of(wb_ref.shape[1:], 0, S5_GROUP_CH) == group_of(wb_ref.shape[1:], 1, S5_STATE))
        for h in range(HALVES):
            wb_ref[h] = jnp.where(keep, _dot(cb_ref[h], e_ref[...]), 0.0).astype(_BF16)

    @pl.when(jnp.maximum(s - 1, 0) % tiles_per_block == 0)
    def _():
        keep = (group_of(wc_ref.shape[1:], 0, S5_STATE) == group_of(wc_ref.shape[1:], 1, S5_GROUP_CH))
        for h in range(HALVES):
            wc_ref[h] = jnp.where(keep, _dot(et_ref[...], cc_ref[h]), 0.0).astype(_BF16)

    @pl.when(s == 0)
    def _():
        bu_ref[...] = jnp.zeros_like(bu_ref)
        uo_ref[...] = jnp.zeros_like(uo_ref)

    @pl.when(jnp.maximum(s - 1, 0) % tiles_per_block == 0)
    def _():
        st_ref[...] = jnp.zeros_like(st_ref)

    _cast_all(cast_src, cast_dst)
    un_ref[...] = _dot(hn_ref[...], wu_ref[...])

    for h in range(HALVES):
        for g in range(w // SCAN_LANES):
            re = slice(g * SCAN_LANES, (g + 1) * SCAN_LANES)
            im = slice(w + g * SCAN_LANES, w + (g + 1) * SCAN_LANES)
            lr = jnp.broadcast_to(lam_ref[h, :, re], shape)
            li = jnp.broadcast_to(lam_ref[h, :, im], shape)

            def advance(hr, hi, l):
                rows = slice(l * SUBLANES, (l + 1) * SUBLANES)
                return (lr * hr - li * hi + bu_ref[h, rows, re],
                        lr * hi + li * hr + bu_ref[h, rows, im])

            er = ei = jnp.where(s < 0, tok_ref[...], 0.0)
            for l in range(SCAN_CHUNK):
                er, ei = advance(er, ei, l)

            pr = p64_ref[h, :, re]
            pi = p64_ref[h, :, im]
            cr = st_ref[h, 0:1, re]
            ci = st_ref[h, 0:1, im]
            rows_r, rows_i = [], []
            for j in range(SUBLANES):
                rows_r.append(cr)
                rows_i.append(ci)
                cr, ci = (er[j:j + 1] + (pr * cr - pi * ci),
                          ei[j:j + 1] + (pr * ci + pi * cr))
            st_ref[h, 0:1, re] = cr
            st_ref[h, 0:1, im] = ci
            hr = jnp.concatenate(rows_r, axis=0)
            hi = jnp.concatenate(rows_i, axis=0)

            for l in range(0, SCAN_CHUNK, 2):
                ar, ai = advance(hr, hi, l)
                hr, hi = advance(ar, ai, l + 1)
                rows = slice(l * SUBLANES, (l + 2) * SUBLANES)
                hb_ref[h, rows, re] = jnp.concatenate([ar, hr], axis=0).astype(_BF16)
                hb_ref[h, rows, im] = jnp.concatenate([ai, hi], axis=0).astype(_BF16)
            tok_ref[...] = hr

    y = jnp.concatenate([_dot(hb_ref[h], wc_ref[h]) for h in range(HALVES)], axis=1)
    o_ref[...] = jax.nn.gelu(y + d_ref[...] * uo_ref[...]).astype(_BF16)
    u = un_ref[...]
    uo_ref[...] = u
    ub = u.astype(_BF16)
    for h in range(HALVES):
        bu_ref[h] = _dot(ub[:, h * HALF_CH:(h + 1) * HALF_CH], wb_ref[h])


def _s5(hn, w_ub, cb, cc, lam, p64, d_skip, cast_jobs):
    s, d = hn.shape
    s5_ch = w_ub.shape[1]
    bc = HALVES * HALF_CH
    nb, ni = s5_ch // bc, s // PERM_TILE
    tiles = nb * ni
    cur = lambda t: jnp.minimum(t, tiles - 1)
    prev = lambda t: jnp.maximum(t - 1, 0)
    cast_in, cast_out, cast_shapes = _cast_specs(cast_jobs, tiles, cur)
    spread = _spread_matrix()
    outs = pl.pallas_call(
        functools.partial(_s5_kernel, n_src=len(cast_in), n_dst=len(cast_out),
                          tiles_per_block=ni, n_tiles=tiles),
        grid=(tiles + 1,),
        in_specs=[pl.BlockSpec((PERM_TILE, d), lambda t: (cur(t) % ni, 0)),
                  _resident((d, bc), lambda t: (0, cur(t) // ni)),
                  _resident((HALVES, HALF_CH, 2 * S5_STATE), lambda t: (cur(t) // ni, 0, 0)),
                  _resident((HALVES, 2 * S5_STATE, HALF_CH), lambda t: (prev(t) // ni, 0, 0)),
                  _resident((HALVES, 1, 2 * HALF_ST), lambda t: (prev(t) // ni, 0, 0)),
                  _resident((HALVES, 1, 2 * HALF_ST), lambda t: (prev(t) // ni, 0, 0)),
                  _resident((1, bc), lambda t: (0, prev(t) // ni)),
                  _resident(spread.shape, lambda t: (0, 0)),
                  _resident(spread.T.shape, lambda t: (0, 0))] + cast_in,
        out_specs=[pl.BlockSpec((PERM_TILE, bc),
                                lambda t: (prev(t) % ni, prev(t) // ni))] + cast_out,
        out_shape=[jax.ShapeDtypeStruct((s, s5_ch), _BF16)] + cast_shapes,
        scratch_shapes=[pltpu.VMEM((HALVES, PERM_TILE, 2 * HALF_ST), _F32),
                        pltpu.VMEM((HALVES, PERM_TILE, 2 * HALF_ST), _BF16),
                        pltpu.VMEM((PERM_TILE, bc), _F32),
                        pltpu.VMEM((PERM_TILE, bc), _F32),
                        pltpu.VMEM((HALVES, SUBLANES, 2 * HALF_ST), _F32),
                        pltpu.VMEM((SUBLANES, SCAN_LANES), _F32),
                        pltpu.VMEM((HALVES, HALF_CH, 2 * HALF_ST), _BF16),
                        pltpu.VMEM((HALVES, 2 * HALF_ST, HALF_CH), _BF16)],
        compiler_params=_params("arbitrary"),
        name="branch_b_s5",
    )(hn, w_ub, cb, cc, lam, p64, d_skip, jnp.asarray(spread, _BF16), jnp.asarray(spread.T, _BF16),
      *[w for job in cast_jobs for w in job])
    return outs[0], outs[1:]


def _s5_params(lam_re, lam_im, log_dt, b_re, b_im, c_re, c_im):
    g = lam_re.shape[0]
    nh = g // GROUPS_PER_HALF
    dt = jnp.exp(log_dt)[:, None]
    a, b = lam_re * dt, lam_im * dt
    mag = jnp.exp(a)
    lbr, lbi = mag * jnp.cos(b), mag * jnp.sin(b)
    den = lam_re * lam_re + lam_im * lam_im
    qr = ((lbr - 1.0) * lam_re + lbi * lam_im) / den
    qi = (lbi * lam_re - (lbr - 1.0) * lam_im) / den
    bbr = qr[..., None] * b_re - qi[..., None] * b_im
    bbi = qr[..., None] * b_im + qi[..., None] * b_re
    mag64 = jnp.exp(SCAN_CHUNK * a)
    p64r, p64i = mag64 * jnp.cos(SCAN_CHUNK * b), mag64 * jnp.sin(SCAN_CHUNK * b)

    gp = GROUPS_PER_HALF
    cb = jnp.stack([bbr, bbi], axis=1).reshape(nh, gp, 2, S5_STATE, S5_GROUP_CH)
    cb = cb.transpose(0, 1, 4, 2, 3).reshape(nh, HALF_CH, 2 * S5_STATE)
    cc = jnp.stack([c_re, -c_im], axis=1).reshape(nh, gp, 2, S5_GROUP_CH, S5_STATE)
    cc = cc.transpose(0, 2, 4, 1, 3).reshape(nh, 2 * S5_STATE, HALF_CH)
    lanes = lambda r, m: jnp.concatenate([r.reshape(nh, 1, HALF_ST), m.reshape(nh, 1, HALF_ST)], axis=2)
    return cb.astype(_BF16), cc.astype(_BF16), lanes(lbr, lbi), lanes(p64r, p64i)


def _spread_matrix():
    e = np.zeros((2 * S5_STATE, 2 * HALF_ST), np.float32)
    for ri in range(2):
        for g in range(GROUPS_PER_HALF):
            for n in range(S5_STATE):
                e[ri * S5_STATE + n, ri * HALF_ST + g * S5_STATE + n] = 1.0
    return e


def _glu_kernel(g_ref, gj_ref, hn_ref, wg_ref, bg_ref, wz_ref, o_ref, t_ref):
    t_ref[...] = gj_ref[...].astype(_F32) * _silu(_dot(hn_ref[...], wz_ref[...]))
    gate = _sigmoid(_dot(g_ref[...], wg_ref[...]) + bg_ref[...])
    o_ref[...] = (t_ref[...] * gate).astype(_BF16)


def _glu(g, hn, w_glu, b_glu, w_in, zb_col0, tm, tn):
    s, c = g.shape
    d = hn.shape[1]
    off = zb_col0 // tn
    return pl.pallas_call(
        _glu_kernel,
        grid=(c // tn, s // tm),
        in_specs=[pl.BlockSpec((tm, c), lambda j, i: (i, 0)),
                  pl.BlockSpec((tm, tn), lambda j, i: (i, j)),
                  pl.BlockSpec((tm, d), lambda j, i: (i, 0)),
                  _resident((c, tn), lambda j, i: (0, j)),
                  _resident((1, tn), lambda j, i: (0, j)),
                  _resident((d, tn), lambda j, i: (0, off + j))],
        out_specs=pl.BlockSpec((tm, tn), lambda j, i: (i, j)),
        out_shape=jax.ShapeDtypeStruct((s, c), _BF16),
        scratch_shapes=[pltpu.VMEM((tm, tn), _F32)],
        compiler_params=_params("parallel", "parallel"),
        name="branch_b_glu",
    )(g, g, hn, w_glu, b_glu, w_in)


def _merge_kernel(hn_ref, ya_ref, yb_ref, wga_ref, wgb_ref, wua_ref, wub_ref, o_ref,
                  m_ref, gs_ref):
    hn = hn_ref[...]
    n_slabs = m_ref.shape[0]
    m = _sigmoid(_dot(hn, wga_ref[...])) * _dot(ya_ref[...], wua_ref[...])
    for k in range(n_slabs):
        m_ref[k] = m[:, k * LANES:(k + 1) * LANES]
    gs_ref[...] = _sigmoid(_dot(hn, wgb_ref[...]))
    m = gs_ref[...] * _dot(yb_ref[...], wub_ref[...])
    for k in range(n_slabs):
        m_ref[k] += m[:, k * LANES:(k + 1) * LANES]
    for j in range(SUBLANES):
        for k in range(n_slabs):
            rows = m_ref[k, pl.ds(j, SCAN_CHUNK, stride=SUBLANES), :]
            o_ref[j * SCAN_CHUNK:(j + 1) * SCAN_CHUNK, k * LANES:(k + 1) * LANES] = rows.astype(_BF16)


def _merge(hn, ya, yb, w_in, ga_col0, w_up_a, w_up_b, tn):
    s, d = hn.shape
    ca, cb = ya.shape[1], yb.shape[1]
    nj = d // tn
    off_a = ga_col0 // tn
    off_b = off_a + nj
    return pl.pallas_call(
        _merge_kernel,
        grid=(nj, s // PERM_TILE),
        in_specs=[pl.BlockSpec((PERM_TILE, d), lambda j, i: (i, 0)),
                  pl.BlockSpec((PERM_TILE, ca), lambda j, i: (i, 0)),
                  pl.BlockSpec((PERM_TILE, cb), lambda j, i: (i, 0)),
                  _resident((d, tn), lambda j, i: (0, off_a + j)),
                  _resident((d, tn), lambda j, i: (0, off_b + j)),
                  _resident((ca, tn), lambda j, i: (0, j)),
                  _resident((cb, tn), lambda j, i: (0, j))],
        out_specs=pl.BlockSpec((PERM_TILE, tn), lambda j, i: (i, j)),
        out_shape=jax.ShapeDtypeStruct((s, d), _BF16),
        scratch_shapes=[pltpu.VMEM((tn // LANES, PERM_TILE, LANES), _F32),
                        pltpu.VMEM((PERM_TILE, tn), _F32)],
        compiler_params=_params("parallel", "parallel"),
        name="merge_unpermute",
    )(hn, ya, yb, w_in, w_in, w_up_a, w_up_b)


def _out_kernel(m_ref, x_ref, w_ref, p_ref, wpp_ref, gp_ref, gf_ref, o_ref,
                hp_ref, ss1_ref, ss2_ref, *, n1, tn, final_norm):
    j = pl.program_id(1)
    nj = pl.num_programs(1)
    d = o_ref.shape[1]

    @pl.when(j == 0)
    def _():
        ss1_ref[...] = jnp.zeros_like(ss1_ref)
        ss2_ref[...] = jnp.zeros_like(ss2_ref)

    @pl.when(j < n1)
    def _():
        cols = pl.ds(pl.multiple_of(j * tn, tn), tn)
        h1 = x_ref[...] + _dot(m_ref[...], w_ref[...])
        o_ref[:, cols] = h1
        hp_ref[:, cols] = (h1 * gp_ref[:, cols]).astype(_BF16)
        ss1_ref[...] += jnp.sum(h1 * h1, axis=-1, keepdims=True)

    @pl.when(j >= n1)
    def _():
        cols = pl.ds(pl.multiple_of((j - n1) * tn, tn), tn)
        rinv = lax.rsqrt(ss1_ref[...] * (1.0 / d) + EPS)
        gate = _sigmoid(rinv * _dot(hp_ref[...], w_ref[...]))
        emb = _dot(p_ref[...].astype(_BF16), wpp_ref[...])
        h2 = o_ref[:, cols] + gate * emb
        o_ref[:, cols] = h2
        ss2_ref[...] += jnp.sum(h2 * h2, axis=-1, keepdims=True)

    if final_norm:
        @pl.when(j == nj - 1)
        def _():
            rinv = lax.rsqrt(ss2_ref[...] * (1.0 / d) + EPS)
            o_ref[...] = o_ref[...] * rinv * gf_ref[...]


def _out(merged, x2, w_cat, p2, w_pp, g_ple, g_final, tm, tn, final_norm):
    s, d = x2.shape
    pd = p2.shape[1]
    n1 = d // tn
    first = lambda j: jnp.minimum(j, n1 - 1)
    second = lambda j: jnp.maximum(j - n1, 0)
    return pl.pallas_call(
        functools.partial(_out_kernel, n1=n1, tn=tn, final_norm=final_norm),
        grid=(s // tm, 2 * n1),
        in_specs=[pl.BlockSpec((tm, d), lambda i, j: (i, 0)),
                  pl.BlockSpec((tm, tn), lambda i, j: (i, first(j))),
                  pl.BlockSpec((d, tn), lambda i, j: (0, j)),
                  pl.BlockSpec((tm, pd), lambda i, j: (i, 0)),
                  pl.BlockSpec((pd, tn), lambda i, j: (0, second(j))),
                  pl.BlockSpec((1, d), lambda i, j: (0, 0)),
                  pl.BlockSpec((1, d), lambda i, j: (0, 0))],
        out_specs=pl.BlockSpec((tm, d), lambda i, j: (i, 0)),
        out_shape=jax.ShapeDtypeStruct((s, d), _F32),
        scratch_shapes=[pltpu.VMEM((tm, d), _BF16),
                        pltpu.VMEM((tm, 1), _F32),
                        pltpu.VMEM((tm, 1), _F32)],
        compiler_params=_params("parallel", "arbitrary"),
        name="out_proj_ple_norm",
    )(merged, x2, w_cat, p2, w_pp, g_ple, g_final)


def kernel(x, p, norm_in_g, w_in, conv_w, conv_b, lam_re, lam_im, log_dt, b_re, b_im, c_re, c_im,
           d_skip, w_glu, b_glu, w_up_conv, w_up_s5, w_out, ple_norm_g, w_ple_gate, w_ple_proj,
           final_norm_g):
    bsz, seq, d = x.shape
    depth = w_in.shape[0]
    conv_ch = conv_w.shape[-1]
    s5_ch = d_skip.shape[-1]
    assert bsz == 1 and seq % PERM_TILE == 0 and s5_ch % (HALVES * HALF_CH) == 0
    col_ub = 4 * conv_ch
    col_zb = col_ub + s5_ch
    col_ga = col_zb + s5_ch
    tm_big = 2 * PERM_TILE if seq % (2 * PERM_TILE) == 0 else PERM_TILE
    tn = min(512, conv_ch, s5_ch)
    tn_wide = min(1024, d)

    pm = jnp.asarray(_perm_matrix(), _BF16)

    h = x.reshape(seq, d)
    for i in range(depth):
        last = i == depth - 1
        cb, cc, lam_l, p64 = _s5_params(lam_re[i], lam_im[i], log_dt[i], b_re[i], b_im[i],
                                        c_re[i], c_im[i])
        hn = _prep(h, norm_in_g[i][None, :], pm)

        w_ub = w_in[i][:, col_ub:col_zb].astype(_BF16)
        g, (w_in_b,) = _s5(hn, w_ub, cb, cc, lam_l, p64, d_skip[i][None, :], [(w_in[i],)])
        ya, (w_glu_b, w_ua_b, w_us_b, w_cat) = _branch_a(
            hn, w_in_b, conv_w[i], conv_b[i][None, :], conv_ch, tm_big, tn,
            [(w_glu[i],), (w_up_conv[i],), (w_up_s5[i],), (w_out[i], w_ple_gate[i])])

        yb = _glu(g, hn, w_glu_b, b_glu[i][None, :], w_in_b, col_zb, tm_big, tn)
        merged = _merge(hn, ya, yb, w_in_b, col_ga, w_ua_b, w_us_b, tn_wide)
        h = _out(merged, h, w_cat, p[i].reshape(seq, -1), w_ple_proj[i].astype(_BF16),
                 ple_norm_g[i][None, :], final_norm_g[None, :], PERM_TILE, tn_wide, last)
    return h.reshape(bsz, seq, d)
```

```python
import functools

import numpy as np
import jax
import jax.numpy as jnp
from jax import lax
from jax.experimental import pallas as pl
from jax.experimental.pallas import tpu as pltpu

EPS = 1e-6
LANES = 128
SUBLANES = 8
BF16_ROWS = 16
SCAN_CHUNK = 64
PERM_TILE = SUBLANES * SCAN_CHUNK
S5_GROUP_CH = 16
S5_STATE = 64
GROUPS_PER_HALF = 16
HALF_CH = GROUPS_PER_HALF * S5_GROUP_CH
HALF_ST = GROUPS_PER_HALF * S5_STATE
HALVES = 2
SCAN_LANES = 512
VMEM_LIMIT = 60 * 1024 * 1024

_BF16 = jnp.bfloat16
_F32 = jnp.float32


def _dot(a, b):
    return jnp.dot(a, b, preferred_element_type=_F32)


def _sigmoid(v):
    return jax.nn.sigmoid(v)


def _silu(v):
    return v * jax.nn.sigmoid(v)


def _perm_matrix():
    pm = np.zeros((PERM_TILE, PERM_TILE), np.float32)
    for j in range(SUBLANES):
        for l in range(SCAN_CHUNK):
            pm[l * SUBLANES + j, j * SCAN_CHUNK + l] = 1.0
    return pm


def _params(*sem):
    return pltpu.CompilerParams(dimension_semantics=sem, vmem_limit_bytes=VMEM_LIMIT)


def _resident(shape, index_map):
    return pl.BlockSpec(shape, index_map, pipeline_mode=pl.Buffered(1))


def _cast_specs(jobs, steps, slab_index):
    in_specs, out_specs, out_shapes = [], [], []
    slab = lambda *idx: (slab_index(*idx), 0)
    for job in jobs:
        rows = job[0].shape[0]
        assert rows % (steps * BF16_ROWS) == 0 and all(w.shape[0] == rows for w in job)
        cols = sum(w.shape[1] for w in job)
        in_specs += [pl.BlockSpec((rows // steps, w.shape[1]), slab) for w in job]
        out_specs.append(pl.BlockSpec((rows // steps, cols), slab))
        out_shapes.append(jax.ShapeDtypeStruct((rows, cols), _BF16))
    return in_specs, out_specs, out_shapes


def _cast_all(src_refs, dst_refs):
    src = iter(src_refs)
    for dst in dst_refs:
        col = 0
        while col < dst.shape[1]:
            w = next(src)
            dst[:, col:col + w.shape[1]] = w[...].astype(_BF16)
            col += w.shape[1]


def _prep_kernel(x_ref, g_ref, pm_ref, o_ref):
    x = x_ref[...]
    ms = jnp.mean(x * x, axis=-1, keepdims=True)
    hn = (x * lax.rsqrt(ms + EPS) * g_ref[...]).astype(_BF16)
    o_ref[...] = _dot(pm_ref[...], hn).astype(_BF16)


def _prep(x2, gain, pm):
    s, d = x2.shape
    return pl.pallas_call(
        _prep_kernel,
        grid=(s // PERM_TILE,),
        in_specs=[pl.BlockSpec((PERM_TILE, d), lambda i: (i, 0)),
                  pl.BlockSpec((1, d), lambda i: (0, 0)),
                  pl.BlockSpec((PERM_TILE, PERM_TILE), lambda i: (0, 0))],
        out_specs=pl.BlockSpec((PERM_TILE, d), lambda i: (i, 0)),
        out_shape=jax.ShapeDtypeStruct((s, d), _BF16),
        compiler_params=_params("parallel"),
        name="prep_norm_permute",
    )(x2, gain, pm)


def _branch_a_kernel(*refs, n_src, n_dst):
    hn_ref, wx_ref, wc_ref, wb_ref, wz_ref, cw_ref, cb_ref = refs[:7]
    cast_src = refs[7:7 + n_src]
    o_ref = refs[7 + n_src]
    cast_dst = refs[8 + n_src:8 + n_src + n_dst]
    v_ref, tail_ref = refs[8 + n_src + n_dst:]
    i = pl.program_id(1)

    @pl.when(i == 0)
    def _():
        tail_ref[...] = jnp.zeros_like(tail_ref)

    _cast_all(cast_src, cast_dst)
    hn = hn_ref[...]
    v_ref[...] = _dot(hn, wc_ref[...]) * _dot(hn, wx_ref[...])
    za = _dot(hn, wz_ref[...])
    tm, tn = v_ref.shape
    sub = lax.broadcasted_iota(jnp.int32, (SUBLANES, tn), 0)
    cw = cw_ref[...]

    def from_previous_chunk(cur, prev_last):
        return jnp.where(sub == 0, prev_last, pltpu.roll(cur, 1, 0))

    for t in range(tm // PERM_TILE):
        r0 = t * PERM_TILE
        v = v_ref[r0:r0 + PERM_TILE, :]
        tail = v[PERM_TILE - 2 * SUBLANES:, :]
        prev = tail_ref[...]
        r62 = from_previous_chunk(tail[:SUBLANES], prev[SUBLANES - 1:SUBLANES])
        r63 = from_previous_chunk(tail[SUBLANES:], prev[2 * SUBLANES - 1:])
        tail_ref[...] = tail
        vs1 = jnp.concatenate([r63, v[:PERM_TILE - SUBLANES]], axis=0)
        vs2 = jnp.concatenate([r62, r63, v[:PERM_TILE - 2 * SUBLANES]], axis=0)
        conv = cw[0:1] * vs2 + cw[1:2] * vs1 + cw[2:3] * v + cb_ref[...]
        v_ref[r0:r0 + PERM_TILE, :] = conv * _silu(za[r0:r0 + PERM_TILE, :])
    o_ref[...] = (_dot(hn, wb_ref[...]) * v_ref[...]).astype(_BF16)


def _branch_a(hn, w_in, conv_w, conv_b, conv_ch, tm, tn, cast_jobs):
    s, d = hn.shape
    nj, ni = conv_ch // tn, s // tm
    wspec = lambda k: _resident((d, tn), lambda j, i, k=k: (0, k * nj + j))
    cast_in, cast_out, cast_shapes = _cast_specs(cast_jobs, nj * ni, lambda j, i: j * ni + i)
    outs = pl.pallas_call(
        functools.partial(_branch_a_kernel, n_src=len(cast_in), n_dst=len(cast_out)),
        grid=(nj, ni),
        in_specs=[pl.BlockSpec((tm, d), lambda j, i: (i, 0)),
                  wspec(0), wspec(1), wspec(2), wspec(3),
                  _resident((conv_w.shape[0], tn), lambda j, i: (0, j)),
                  _resident((1, tn), lambda j, i: (0, j))] + cast_in,
        out_specs=[pl.BlockSpec((tm, tn), lambda j, i: (i, j))] + cast_out,
        out_shape=[jax.ShapeDtypeStruct((s, conv_ch), _BF16)] + cast_shapes,
        scratch_shapes=[pltpu.VMEM((tm, tn), _F32),
                        pltpu.VMEM((2 * SUBLANES, tn), _F32)],
        compiler_params=_params("arbitrary", "arbitrary"),
        name="branch_a_conv",
    )(hn, w_in, w_in, w_in, w_in, conv_w, conv_b, *[w for job in cast_jobs for w in job])
    return outs[0], outs[1:]


def _s5_kernel(*refs, n_src, n_dst, tiles_per_block, n_tiles):
    hn_ref, wu_ref, cb_ref, cc_ref, lam_ref, p64_ref, d_ref, e_ref, et_ref = refs[:9]
    cast_src = refs[9:9 + n_src]
    o_ref = refs[9 + n_src]
    cast_dst = refs[10 + n_src:10 + n_src + n_dst]
    bu_ref, hb_ref, un_ref, uo_ref, st_ref, tok_ref, wb_ref, wc_ref = refs[10 + n_src + n_dst:]
    s = pl.program_id(0)
    w = HALF_ST
    shape = (SUBLANES, SCAN_LANES)
    tok_ref[...] = jnp.zeros_like(tok_ref)

    def group_of(shape, axis, per_group):
        idx = lax.broadcasted_iota(jnp.int32, shape, axis)
        return (idx % (GROUPS_PER_HALF * per_group)) // per_group

    @pl.when(jnp.minimum(s, n_tiles - 1) % tiles_per_block == 0)
    def _():
        keep = (group_of(wb_ref.shape[1:], 0, S5_GROUP_CH) == group_of(wb_ref.shape[1:], 1, S5_STATE))
        for h in range(HALVES):
            wb_ref[h] = jnp.where(keep, _dot(cb_ref[h], e_ref[...]), 0.0).astype(_BF16)

    @pl.when(jnp.maximum(s - 1, 0) % tiles_per_block == 0)
    def _():
        keep = (group_of(wc_ref.shape[1:], 0, S5_STATE) == group_of(wc_ref.shape[1:], 1, S5_GROUP_CH))
        for h in range(HALVES):
            wc_ref[h] = jnp.where(keep, _dot(et_ref[...], cc_ref[h]), 0.0).astype(_BF16)

    @pl.when(s == 0)
    def _():
        bu_ref[...] = jnp.zeros_like(bu_ref)
        uo_ref[...] = jnp.zeros_like(uo_ref)

    @pl.when(jnp.maximum(s - 1, 0) % tiles_per_block == 0)
    def _():
        st_ref[...] = jnp.zeros_like(st_ref)

    _cast_all(cast_src, cast_dst)
    un_ref[...] = _dot(hn_ref[...], wu_ref[...])

    for h in range(HALVES):
        for g in range(w // SCAN_LANES):
            re = slice(g * SCAN_LANES, (g + 1) * SCAN_LANES)
            im = slice(w + g * SCAN_LANES, w + (g + 1) * SCAN_LANES)
            lr = jnp.broadcast_to(lam_ref[h, :, re], shape)
            li = jnp.broadcast_to(lam_ref[h, :, im], shape)

            def advance(hr, hi, l):
                rows = slice(l * SUBLANES, (l + 1) * SUBLANES)
                return (lr * hr - li * hi + bu_ref[h, rows, re],
                        lr * hi + li * hr + bu_ref[h, rows, im])

            er = ei = jnp.where(s < 0, tok_ref[...], 0.0)
            for l in range(SCAN_CHUNK):
                er, ei = advance(er, ei, l)

            pr = p64_ref[h, :, re]
            pi = p64_ref[h, :, im]
            cr = st_ref[h, 0:1, re]
            ci = st_ref[h, 0:1, im]
            rows_r, rows_i = [], []
            for j in range(SUBLANES):
                rows_r.append(cr)
                rows_i.append(ci)
                cr, ci = (er[j:j + 1] + (pr * cr - pi * ci),
                          ei[j:j + 1] + (pr * ci + pi * cr))
            st_ref[h, 0:1, re] = cr
            st_ref[h, 0:1, im] = ci
            hr = jnp.concatenate(rows_r, axis=0)
            hi = jnp.concatenate(rows_i, axis=0)

            for l in range(0, SCAN_CHUNK, 2):
                ar, ai = advance(hr, hi, l)
                hr, hi = advance(ar, ai, l + 1)
                rows = slice(l * SUBLANES, (l + 2) * SUBLANES)
                hb_ref[h, rows, re] = jnp.concatenate([ar, hr], axis=0).astype(_BF16)
                hb_ref[h, rows, im] = jnp.concatenate([ai, hi], axis=0).astype(_BF16)
            tok_ref[...] = hr

    y = jnp.concatenate([_dot(hb_ref[h], wc_ref[h]) for h in range(HALVES)], axis=1)
    o_ref[...] = jax.nn.gelu(y + d_ref[...] * uo_ref[...]).astype(_BF16)
    u = un_ref[...]
    uo_ref[...] = u
    ub = u.astype(_BF16)
    for h in range(HALVES):
        bu_ref[h] = _dot(ub[:, h * HALF_CH:(h + 1) * HALF_CH], wb_ref[h])


def _s5(hn, w_ub, cb, cc, lam, p64, d_skip, cast_jobs):
    s, d = hn.shape
    s5_ch = w_ub.shape[1]
    bc = HALVES * HALF_CH
    nb, ni = s5_ch // bc, s // PERM_TILE
    tiles = nb * ni
    cur = lambda t: jnp.minimum(t, tiles - 1)
    prev = lambda t: jnp.maximum(t - 1, 0)
    cast_in, cast_out, cast_shapes = _cast_specs(cast_jobs, tiles, cur)
    spread = _spread_matrix()
    outs = pl.pallas_call(
        functools.partial(_s5_kernel, n_src=len(cast_in), n_dst=len(cast_out),
                          tiles_per_block=ni, n_tiles=tiles),
        grid=(tiles + 1,),
        in_specs=[pl.BlockSpec((PERM_TILE, d), lambda t: (cur(t) % ni, 0)),
                  _resident((d, bc), lambda t: (0, cur(t) // ni)),
                  _resident((HALVES, HALF_CH, 2 * S5_STATE), lambda t: (cur(t) // ni, 0, 0)),
                  _resident((HALVES, 2 * S5_STATE, HALF_CH), lambda t: (prev(t) // ni, 0, 0)),
                  _resident((HALVES, 1, 2 * HALF_ST), lambda t: (prev(t) // ni, 0, 0)),
                  _resident((HALVES, 1, 2 * HALF_ST), lambda t: (prev(t) // ni, 0, 0)),
                  _resident((1, bc), lambda t: (0, prev(t) // ni)),
                  _resident(spread.shape, lambda t: (0, 0)),
                  _resident(spread.T.shape, lambda t: (0, 0))] + cast_in,
        out_specs=[pl.BlockSpec((PERM_TILE, bc),
                                lambda t: (prev(t) % ni, prev(t) // ni))] + cast_out,
        out_shape=[jax.ShapeDtypeStruct((s, s5_ch), _BF16)] + cast_shapes,
        scratch_shapes=[pltpu.VMEM((HALVES, PERM_TILE, 2 * HALF_ST), _F32),
                        pltpu.VMEM((HALVES, PERM_TILE, 2 * HALF_ST), _BF16),
                        pltpu.VMEM((PERM_TILE, bc), _F32),
                        pltpu.VMEM((PERM_TILE, bc), _F32),
                        pltpu.VMEM((HALVES, SUBLANES, 2 * HALF_ST), _F32),
                        pltpu.VMEM((SUBLANES, SCAN_LANES), _F32),
                        pltpu.VMEM((HALVES, HALF_CH, 2 * HALF_ST), _BF16),
                        pltpu.VMEM((HALVES, 2 * HALF_ST, HALF_CH), _BF16)],
        compiler_params=_params("arbitrary"),
        name="branch_b_s5",
    )(hn, w_ub, cb, cc, lam, p64, d_skip, jnp.asarray(spread, _BF16), jnp.asarray(spread.T, _BF16),
      *[w for job in cast_jobs for w in job])
    return outs[0], outs[1:]


def _s5_params(lam_re, lam_im, log_dt, b_re, b_im, c_re, c_im):
    g = lam_re.shape[0]
    nh = g // GROUPS_PER_HALF
    dt = jnp.exp(log_dt)[:, None]
    a, b = lam_re * dt, lam_im * dt
    mag = jnp.exp(a)
    lbr, lbi = mag * jnp.cos(b), mag * jnp.sin(b)
    den = lam_re * lam_re + lam_im * lam_im
    qr = ((lbr - 1.0) * lam_re + lbi * lam_im) / den
    qi = (lbi * lam_re - (lbr - 1.0) * lam_im) / den
    bbr = qr[..., None] * b_re - qi[..., None] * b_im
    bbi = qr[..., None] * b_im + qi[..., None] * b_re
    mag64 = jnp.exp(SCAN_CHUNK * a)
    p64r, p64i = mag64 * jnp.cos(SCAN_CHUNK * b), mag64 * jnp.sin(SCAN_CHUNK * b)

    gp = GROUPS_PER_HALF
    cb = jnp.stack([bbr, bbi], axis=1).reshape(nh, gp, 2, S5_STATE, S5_GROUP_CH)
    cb = cb.transpose(0, 1, 4, 2, 3).reshape(nh, HALF_CH, 2 * S5_STATE)
    cc = jnp.stack([c_re, -c_im], axis=1).reshape(nh, gp, 2, S5_GROUP_CH, S5_STATE)
    cc = cc.transpose(0, 2, 4, 1, 3).reshape(nh, 2 * S5_STATE, HALF_CH)
    lanes = lambda r, m: jnp.concatenate([r.reshape(nh, 1, HALF_ST), m.reshape(nh, 1, HALF_ST)], axis=2)
    return cb.astype(_BF16), cc.astype(_BF16), lanes(lbr, lbi), lanes(p64r, p64i)


def _spread_matrix():
    e = np.zeros((2 * S5_STATE, 2 * HALF_ST), np.float32)
    for ri in range(2):
        for g in range(GROUPS_PER_HALF):
            for n in range(S5_STATE):
                e[ri * S5_STATE + n, ri * HALF_ST + g * S5_STATE + n] = 1.0
    return e


def _glu_kernel(g_ref, gj_ref, hn_ref, wg_ref, bg_ref, wz_ref, o_ref, t_ref):
    t_ref[...] = gj_ref[...].astype(_F32) * _silu(_dot(hn_ref[...], wz_ref[...]))
    gate = _sigmoid(_dot(g_ref[...], wg_ref[...]) + bg_ref[...])
    o_ref[...] = (t_ref[...] * gate).astype(_BF16)


def _glu(g, hn, w_glu, b_glu, w_in, zb_col0, tm, tn):
    s, c = g.shape
    d = hn.shape[1]
    off = zb_col0 // tn
    return pl.pallas_call(
        _glu_kernel,
        grid=(c // tn, s // tm),
        in_specs=[pl.BlockSpec((tm, c), lambda j, i: (i, 0)),
                  pl.BlockSpec((tm, tn), lambda j, i: (i, j)),
                  pl.BlockSpec((tm, d), lambda j, i: (i, 0)),
                  _resident((c, tn), lambda j, i: (0, j)),
                  _resident((1, tn), lambda j, i: (0, j)),
                  _resident((d, tn), lambda j, i: (0, off + j))],
        out_specs=pl.BlockSpec((tm, tn), lambda j, i: (i, j)),
        out_shape=jax.ShapeDtypeStruct((s, c), _BF16),
        scratch_shapes=[pltpu.VMEM((tm, tn), _F32)],
        compiler_params=_params("parallel", "parallel"),
        name="branch_b_glu",
    )(g, g, hn, w_glu, b_glu, w_in)


def _merge_kernel(hn_ref, ya_ref, yb_ref, wga_ref, wgb_ref, wua_ref, wub_ref, o_ref,
                  m_ref, gs_ref):
    hn = hn_ref[...]
    n_slabs = m_ref.shape[0]
    m = _sigmoid(_dot(hn, wga_ref[...])) * _dot(ya_ref[...], wua_ref[...])
    for k in range(n_slabs):
        m_ref[k] = m[:, k * LANES:(k + 1) * LANES]
    gs_ref[...] = _sigmoid(_dot(hn, wgb_ref[...]))
    m = gs_ref[...] * _dot(yb_ref[...], wub_ref[...])
    for k in range(n_slabs):
        m_ref[k] += m[:, k * LANES:(k + 1) * LANES]
    for j in range(SUBLANES):
        for k in range(n_slabs):
            rows = m_ref[k, pl.ds(j, SCAN_CHUNK, stride=SUBLANES), :]
            o_ref[j * SCAN_CHUNK:(j + 1) * SCAN_CHUNK, k * LANES:(k + 1) * LANES] = rows.astype(_BF16)


def _merge(hn, ya, yb, w_in, ga_col0, w_up_a, w_up_b, tn):
    s, d = hn.shape
    ca, cb = ya.shape[1], yb.shape[1]
    nj = d // tn
    off_a = ga_col0 // tn
    off_b = off_a + nj
    return pl.pallas_call(
        _merge_kernel,
        grid=(nj, s // PERM_TILE),
        in_specs=[pl.BlockSpec((PERM_TILE, d), lambda j, i: (i, 0)),
                  pl.BlockSpec((PERM_TILE, ca), lambda j, i: (i, 0)),
                  pl.BlockSpec((PERM_TILE, cb), lambda j, i: (i, 0)),
                  _resident((d, tn), lambda j, i: (0, off_a + j)),
                  _resident((d, tn), lambda j, i: (0, off_b + j)),
                  _resident((ca, tn), lambda j, i: (0, j)),
                  _resident((cb, tn), lambda j, i: (0, j))],
        out_specs=pl.BlockSpec((PERM_TILE, tn), lambda j, i: (i, j)),
        out_shape=jax.ShapeDtypeStruct((s, d), _BF16),
        scratch_shapes=[pltpu.VMEM((tn // LANES, PERM_TILE, LANES), _F32),
                        pltpu.VMEM((PERM_TILE, tn), _F32)],
        compiler_params=_params("parallel", "parallel"),
        name="merge_unpermute",
    )(hn, ya, yb, w_in, w_in, w_up_a, w_up_b)


def _out_kernel(m_ref, x_ref, w_ref, p_ref, wpp_ref, gp_ref, gf_ref, o_ref,
                hp_ref, ss1_ref, ss2_ref, *, n1, tn, final_norm):
    j = pl.program_id(1)
    nj = pl.num_programs(1)
    d = o_ref.shape[1]

    @pl.when(j == 0)
    def _():
        ss1_ref[...] = jnp.zeros_like(ss1_ref)
        ss2_ref[...] = jnp.zeros_like(ss2_ref)

    @pl.when(j < n1)
    def _():
        cols = pl.ds(pl.multiple_of(j * tn, tn), tn)
        h1 = x_ref[...] + _dot(m_ref[...], w_ref[...])
        o_ref[:, cols] = h1
        hp_ref[:, cols] = (h1 * gp_ref[:, cols]).astype(_BF16)
        ss1_ref[...] += jnp.sum(h1 * h1, axis=-1, keepdims=True)

    @pl.when(j >= n1)
    def _():
        cols = pl.ds(pl.multiple_of((j - n1) * tn, tn), tn)
        rinv = lax.rsqrt(ss1_ref[...] * (1.0 / d) + EPS)
        gate = _sigmoid(rinv * _dot(hp_ref[...], w_ref[...]))
        emb = _dot(p_ref[...].astype(_BF16), wpp_ref[...])
        h2 = o_ref[:, cols] + gate * emb
        o_ref[:, cols] = h2
        ss2_ref[...] += jnp.sum(h2 * h2, axis=-1, keepdims=True)

    if final_norm:
        @pl.when(j == nj - 1)
        def _():
            rinv = lax.rsqrt(ss2_ref[...] * (1.0 / d) + EPS)
            o_ref[...] = o_ref[...] * rinv * gf_ref[...]


def _out(merged, x2, w_cat, p2, w_pp, g_ple, g_final, tm, tn, final_norm):
    s, d = x2.shape
    pd = p2.shape[1]
    n1 = d // tn
    first = lambda j: jnp.minimum(j, n1 - 1)
    second = lambda j: jnp.maximum(j - n1, 0)
    return pl.pallas_call(
        functools.partial(_out_kernel, n1=n1, tn=tn, final_norm=final_norm),
        grid=(s // tm, 2 * n1),
        in_specs=[pl.BlockSpec((tm, d), lambda i, j: (i, 0)),
                  pl.BlockSpec((tm, tn), lambda i, j: (i, first(j))),
                  pl.BlockSpec((d, tn), lambda i, j: (0, j)),
                  pl.BlockSpec((tm, pd), lambda i, j: (i, 0)),
                  pl.BlockSpec((pd, tn), lambda i, j: (0, second(j))),
                  pl.BlockSpec((1, d), lambda i, j: (0, 0)),
                  pl.BlockSpec((1, d), lambda i, j: (0, 0))],
        out_specs=pl.BlockSpec((tm, d), lambda i, j: (i, 0)),
        out_shape=jax.ShapeDtypeStruct((s, d), _F32),
        scratch_shapes=[pltpu.VMEM((tm, d), _BF16),
                        pltpu.VMEM((tm, 1), _F32),
                        pltpu.VMEM((tm, 1), _F32)],
        compiler_params=_params("parallel", "arbitrary"),
        name="out_proj_ple_norm",
    )(merged, x2, w_cat, p2, w_pp, g_ple, g_final)


def kernel(x, p, norm_in_g, w_in, conv_w, conv_b, lam_re, lam_im, log_dt, b_re, b_im, c_re, c_im,
           d_skip, w_glu, b_glu, w_up_conv, w_up_s5, w_out, ple_norm_g, w_ple_gate, w_ple_proj,
           final_norm_g):
    bsz, seq, d = x.shape
    depth = w_in.shape[0]
    conv_ch = conv_w.shape[-1]
    s5_ch = d_skip.shape[-1]
    assert bsz == 1 and seq % PERM_TILE == 0 and s5_ch % (HALVES * HALF_CH) == 0
    col_ub = 4 * conv_ch
    col_zb = col_ub + s5_ch
    col_ga = col_zb + s5_ch
    tm_big = 2 * PERM_TILE if seq % (2 * PERM_TILE) == 0 else PERM_TILE
    tn = min(512, conv_ch, s5_ch)
    tn_wide = min(1024, d)

    pm = jnp.asarray(_perm_matrix(), _BF16)

    h = x.reshape(seq, d)
    for i in range(depth):
        last = i == depth - 1
        cb, cc, lam_l, p64 = _s5_params(lam_re[i], lam_im[i], log_dt[i], b_re[i], b_im[i],
                                        c_re[i], c_im[i])
        hn = _prep(h, norm_in_g[i][None, :], pm)

        w_ub = w_in[i][:, col_ub:col_zb].astype(_BF16)
        g, (w_in_b,) = _s5(hn, w_ub, cb, cc, lam_l, p64, d_skip[i][None, :], [(w_in[i],)])
        ya, (w_glu_b, w_ua_b, w_us_b, w_cat) = _branch_a(
            hn, w_in_b, conv_w[i], conv_b[i][None, :], conv_ch, tm_big, tn,
            [(w_glu[i],), (w_up_conv[i],), (w_up_s5[i],), (w_out[i], w_ple_gate[i])])

        yb = _glu(g, hn, w_glu_b, b_glu[i][None, :], w_in_b, col_zb, tm_big, min(tn_wide, s5_ch))
        merged = _merge(hn, ya, yb, w_in_b, col_ga, w_ua_b, w_us_b, tn_wide)
        h = _out(merged, h, w_cat, p[i].reshape(seq, -1), w_ple_proj[i].astype(_BF16),
                 ple_norm_g[i][None, :], final_norm_g[None, :], PERM_TILE, tn_wide, last)
    return h.reshape(bsz, seq, d)
```
